```python
import jax, jax.numpy as jnp
from jax import lax
import numpy as np

D_MODEL = 1024
BATCH = 4
SEQ = 4096
DEPTH = 4

N_Q_A = 8
N_KV_A = 2
HEAD_DIM_A = 64
W_A = N_Q_A * HEAD_DIM_A
W_KV_A = N_KV_A * HEAD_DIM_A
WINDOW = 128
BLOCK = 128
N_HEADS_B = 4
HEAD_DIM_B = 128
W_B = N_HEADS_B * HEAD_DIM_B
N_HEADS_C = 4
DK_C = 128
DV_C = 256
WK_C = N_HEADS_C * DK_C
WV_C = N_HEADS_C * DV_C
GATE_RANK = 16
GATE_TEMP = 16.0
N_MEM = 256
N_HEADS_M = 4
HEAD_DIM_M = 128
W_M = N_HEADS_M * HEAD_DIM_M

CHUNK = 16
EPS = 1e-6
MASK_VALUE = -1e30
MIN_GATE = 1e-30
N_EVEN = (DEPTH + 1) // 2
N_ODD = DEPTH // 2
EVEN_SIZES = (W_A, W_KV_A, W_KV_A, W_A, W_B, W_B, W_B, W_B, W_B, W_M, W_M)
ODD_SIZES = (WK_C, WK_C, WV_C, WV_C, GATE_RANK, GATE_RANK, W_M, W_M)
EVEN_IN = sum(EVEN_SIZES)
ODD_IN = sum(ODD_SIZES)
MIX_EVEN = W_A + W_B + W_M
MIX_ODD = WV_C + W_M

kernel_name = "hybrid_bidir_swa_hgrn2_gla_mem"


def rmsnorm(x, g):
    xf = x.astype(jnp.float32)
    y = xf * lax.rsqrt(jnp.mean(xf * xf, axis=-1, keepdims=True) + EPS)
    return (y * g.astype(jnp.float32)).astype(x.dtype)


def split_cols(t, sizes):
    return jnp.split(t, [int(s) for s in np.cumsum(sizes)[:-1]], axis=-1)


def split_heads(t, n_heads):
    B, T, W = t.shape
    return t.reshape(B, T, n_heads, W // n_heads).transpose(0, 2, 1, 3)


def merge_heads(t):
    B, H, T, d = t.shape
    return t.transpose(0, 2, 1, 3).reshape(B, T, H * d)


def group_rmsnorm(o, g, n_heads):
    B, T, W = o.shape
    y = rmsnorm(o.reshape(B, T, n_heads, W // n_heads), g.reshape(n_heads, W // n_heads))
    return y.reshape(B, T, W)


def alibi_slopes(n):
    return 2.0 ** (-8.0 * jnp.arange(1, n + 1, dtype=jnp.float32) / n)


def window_attention(q, k, v, sink):
    f32 = jnp.float32
    B, Hq, T, d = q.shape
    Hkv = k.shape[1]
    G = Hq // Hkv
    nb = T // BLOCK

    def key_blocks(t):
        tp = jnp.pad(t.astype(f32), ((0, 0), (0, 0), (BLOCK, BLOCK), (0, 0)))
        tp = tp.reshape(B, Hkv, nb + 2, BLOCK, d)
        return jnp.concatenate([tp[:, :, :-2], tp[:, :, 1:-1], tp[:, :, 2:]], axis=3)

    kb, vb = key_blocks(k), key_blocks(v)
    qb = q.astype(f32).reshape(B, Hkv, G, nb, BLOCK, d)
    s = jnp.einsum('bngcid,bncjd->bngcij', qb, kb) * (d ** -0.5)
    i = jnp.arange(BLOCK)[:, None]
    j = jnp.arange(3 * BLOCK)[None, :]
    dist = jnp.abs(i - j + BLOCK).astype(f32)
    kpos = (jnp.arange(nb)[:, None, None] - 1) * BLOCK + j[None]
    valid = (dist <= WINDOW)[None] & (kpos >= 0) & (kpos < T)
    slopes = alibi_slopes(Hq).reshape(Hkv, G, 1, 1, 1)
    s = jnp.where(valid, s - slopes * dist, MASK_VALUE)
    sk = sink.astype(f32).reshape(Hkv, G, 1, 1, 1)
    m = jnp.maximum(jnp.max(s, axis=-1, keepdims=True), sk)
    p = jnp.where(valid, jnp.exp(s - m), 0.0)
    denom = jnp.sum(p, axis=-1, keepdims=True) + jnp.exp(sk - m)
    o = jnp.einsum('bngcij,bncjd->bngcid', p, vb) / denom
    return o.reshape(B, Hq, T, d)


def chunked_gated_scan(q, k, v, log_f):
    f32 = jnp.float32
    B, H, T, dk = q.shape
    dv = v.shape[-1]
    n = T // CHUNK

    def to_chunks(t):
        return t.astype(f32).reshape(B, H, n, CHUNK, t.shape[-1]).transpose(2, 0, 1, 3, 4)

    qc, kc, vc, gc = to_chunks(q), to_chunks(k), to_chunks(v), to_chunks(log_f)
    lower = jnp.tril(jnp.ones((CHUNK, CHUNK), dtype=bool))[:, :, None]

    def step(S, inp):
        qi, ki, vi, gi = inp
        b = jnp.cumsum(gi, axis=-2)
        b_last = b[:, :, -1:, :]
        o_inter = jnp.einsum('bhtk,bhkv->bhtv', qi * jnp.exp(b), S)
        diff = b[:, :, :, None, :] - b[:, :, None, :, :]
        decay = jnp.where(lower, jnp.exp(jnp.where(lower, diff, 0.0)), 0.0)
        A = jnp.einsum('bhtk,bhtsk,bhsk->bhts', qi, decay, ki)
        o_intra = jnp.einsum('bhts,bhsv->bhtv', A, vi)
        S_new = jnp.exp(b_last)[:, :, 0, :, None] * S + jnp.einsum(
            'bhsk,bhsv->bhkv', ki * jnp.exp(b_last - b), vi)
        return S_new, o_inter + o_intra

    S0 = jnp.zeros((B, H, dk, dv), f32)
    _, oc = lax.scan(step, S0, (qc, kc, vc, gc))
    return oc.transpose(1, 2, 0, 3, 4).reshape(B, H, T, dv)


def bidir_scan(q, k_fwd, k_bwd, v, lf_fwd, lf_bwd):
    flip = lambda t: jnp.flip(t, axis=2)
    fwd = chunked_gated_scan(q, k_fwd, v, lf_fwd)
    bwd = flip(chunked_gated_scan(flip(q), flip(k_bwd), flip(v), flip(lf_bwd)))
    return fwd + bwd


def hgrn_forget(z, lb):
    zf = z.astype(jnp.float32)
    f = lb + (1.0 - lb) * jax.nn.sigmoid(zf)
    log_f = jnp.log(jnp.maximum(f, MIN_GATE))
    k = (1.0 - lb) * jax.nn.sigmoid(-zf)
    return log_f, k


def memory_attention(q, mem_n, w_kv):
    f32 = jnp.float32
    k, v = jnp.split(mem_n @ w_kv, 2, axis=-1)
    qh = split_heads(q, N_HEADS_M).astype(f32)
    kh = split_heads(k, N_HEADS_M).astype(f32)
    vh = split_heads(v, N_HEADS_M).astype(f32)
    p = jax.nn.softmax(jnp.einsum('bhtd,bhsd->bhts', qh, kh) * (HEAD_DIM_M ** -0.5), axis=-1)
    return merge_heads(jnp.einsum('bhts,bhsd->bhtd', p, vh)).astype(q.dtype)


def even_layer(x, g_norm, w_in, sink, lb, hgrn_g, w_out, mem_n, w_kv):
    h = rmsnorm(x, g_norm)
    qA, kA, vA, gA, qB, zBf, zBb, iB, gB, qM, gM = split_cols(h @ w_in, EVEN_SIZES)
    a = window_attention(split_heads(qA, N_Q_A), split_heads(kA, N_KV_A),
                         split_heads(vA, N_KV_A), sink)
    a = merge_heads(a).astype(x.dtype) * jax.nn.silu(gA)
    lf_f, k_f = hgrn_forget(zBf, lb[0])
    lf_b, k_b = hgrn_forget(zBb, lb[1])
    sh = lambda t: split_heads(t, N_HEADS_B)
    o = bidir_scan(sh(jax.nn.silu(qB)), sh(k_f), sh(k_b), sh(iB), sh(lf_f), sh(lf_b))
    o = group_rmsnorm(merge_heads(o).astype(x.dtype), hgrn_g, N_HEADS_B) * jax.nn.silu(gB)
    mo = memory_attention(qM, mem_n, w_kv) * jax.nn.silu(gM)
    return jnp.concatenate([a, o, mo], axis=-1) @ w_out


def odd_layer(x, g_norm, w_in, w_up, b_gate, gla_g, w_out, mem_n, w_kv):
    h = rmsnorm(x, g_norm)
    qC, kC, vC, gC, rf, rb, qM, gM = split_cols(h @ w_in, ODD_SIZES)
    lf_f = jax.nn.log_sigmoid((rf @ w_up[0] + b_gate[0]).astype(jnp.float32)) / GATE_TEMP
    lf_b = jax.nn.log_sigmoid((rb @ w_up[1] + b_gate[1]).astype(jnp.float32)) / GATE_TEMP
    sh = lambda t: split_heads(t, N_HEADS_C)
    kh = sh(kC)
    o = bidir_scan(sh(qC * (DK_C ** -0.5)), kh, kh, sh(vC), sh(lf_f), sh(lf_b))
    o = group_rmsnorm(merge_heads(o).astype(x.dtype), gla_g, N_HEADS_C) * jax.nn.silu(gC)
    mo = memory_attention(qM, mem_n, w_kv) * jax.nn.silu(gM)
    return jnp.concatenate([o, mo], axis=-1) @ w_out


def setup_inputs(seed: int = 0) -> dict:
    key = jax.random.key(seed)
    ks = jax.random.split(key, 18)
    nrm = lambda k, shape, scale: jax.random.normal(k, shape, jnp.float32) * scale
    return {
        "x": nrm(ks[0], (BATCH, SEQ, D_MODEL), 1.0),
        "mem": nrm(ks[1], (BATCH, N_MEM, D_MODEL), 1.0),
        "norm_even": 1.0 + nrm(ks[2], (N_EVEN, D_MODEL), 0.02),
        "w_in_even": nrm(ks[3], (N_EVEN, D_MODEL, EVEN_IN), D_MODEL ** -0.5),
        "sink": nrm(ks[4], (N_EVEN, N_Q_A), 0.5),
        "lb_param": nrm(ks[5], (N_EVEN, 2, W_B), 0.5),
        "hgrn_norm": 1.0 + nrm(ks[6], (N_EVEN, W_B), 0.02),
        "w_out_even": nrm(ks[7], (N_EVEN, MIX_EVEN, D_MODEL), MIX_EVEN ** -0.5),
        "norm_odd": 1.0 + nrm(ks[8], (N_ODD, D_MODEL), 0.02),
        "w_in_odd": nrm(ks[9], (N_ODD, D_MODEL, ODD_IN), D_MODEL ** -0.5),
        "w_gate_up": nrm(ks[10], (N_ODD, 2, GATE_RANK, WK_C), GATE_RANK ** -0.5),
        "b_gate": nrm(ks[11], (N_ODD, 2, WK_C), 0.1),
        "gla_norm": 1.0 + nrm(ks[12], (N_ODD, WV_C), 0.02),
        "w_out_odd": nrm(ks[13], (N_ODD, MIX_ODD, D_MODEL), MIX_ODD ** -0.5),
        "mem_norm": 1.0 + nrm(ks[14], (D_MODEL,), 0.02),
        "w_mem_kv": nrm(ks[15], (DEPTH, D_MODEL, 2 * W_M), D_MODEL ** -0.5),
        "final_norm": 1.0 + nrm(ks[16], (D_MODEL,), 0.02),
    }


def reference(x, mem, norm_even, w_in_even, sink, lb_param, hgrn_norm, w_out_even,
              norm_odd, w_in_odd, w_gate_up, b_gate, gla_norm, w_out_odd,
              mem_norm, w_mem_kv, final_norm):
    mem_n = rmsnorm(mem, mem_norm)
    lbs = jax.nn.softmax(lb_param.astype(jnp.float32), axis=0)
    lower = jnp.cumsum(lbs, axis=0) - lbs[0]
    for l in range(DEPTH):
        i = l // 2
        if l % 2 == 0:
            x = x + even_layer(x, norm_even[i], w_in_even[i], sink[i], lower[i],
                               hgrn_norm[i], w_out_even[i], mem_n, w_mem_kv[l])
        else:
            x = x + odd_layer(x, norm_odd[i], w_in_odd[i], w_gate_up[i], b_gate[i],
                              gla_norm[i], w_out_odd[i], mem_n, w_mem_kv[l])
    return rmsnorm(x, final_norm)
```

```python
import functools

import numpy as np
import jax
import jax.numpy as jnp
from jax import lax
from jax.experimental import pallas as pl
from jax.experimental.pallas import tpu as pltpu

F32 = jnp.float32
BF16 = jnp.bfloat16

D_MODEL = 1024
DEPTH = 4
N_Q_A, N_KV_A, HEAD_DIM_A = 8, 2, 64
W_A, W_KV_A = 512, 128
WINDOW = 128
BLOCK = 128
N_HEADS_B, HEAD_DIM_B, W_B = 4, 128, 512
N_HEADS_C, DK_C, DV_C = 4, 128, 256
WK_C, WV_C = 512, 1024
GATE_RANK = 16
GATE_TEMP = 16.0
N_MEM, N_HEADS_M, HEAD_DIM_M, W_M = 256, 4, 128, 512
EPS = 1e-6
MASK_VALUE = -1e30
MIN_GATE = 1e-30
MIX = 1536

EVEN_IN = 4864
E_QA, E_GA, E_QB, E_ZF, E_ZB, E_IB, E_GB, E_QM, E_GM, E_KA, E_VA = (
    0, 512, 1024, 1536, 2048, 2560, 3072, 3584, 4096, 4608, 4736)
ODD_IN_PAD = 4352
O_QC, O_KC, O_VC, O_GC, O_QM, O_GM, O_R = 0, 512, 1024, 2048, 3072, 3584, 4096

LANE = 128
PROJ_DTYPE = F32
PROJ_TM = 256
PROJ_TN = 256
CHUNK = 64
N_LEVELS = 6
MEM_TQ = 512
VMEM_LIMIT = 56 * 1024 * 1024

_NT = (((1,), (1,)), ((), ()))
_TN = (((0,), (0,)), ((), ()))


def _dot(a, b):
    return jnp.dot(a, b, preferred_element_type=F32)


def _dot_nt(a, b):
    return lax.dot_general(a, b, _NT, preferred_element_type=F32)


def _dot_tn(a, b):
    return lax.dot_general(a, b, _TN, preferred_element_type=F32)


def _silu(x):
    return x * jax.nn.sigmoid(x)


def _params(sem):
    return pltpu.CompilerParams(dimension_semantics=sem, vmem_limit_bytes=VMEM_LIMIT)


def _inproj_kernel(x_ref, g_ref, w_ref, o_ref):
    x = x_ref[...]
    ms = jnp.mean(x * x, axis=-1, keepdims=True)
    h = (x * lax.rsqrt(ms + EPS) * g_ref[...]).astype(BF16)
    n_out = o_ref.shape[1]
    for c in range(0, n_out, PROJ_TN):
        o_ref[:, c:c + PROJ_TN] = _dot(h, w_ref[:, c:c + PROJ_TN]).astype(o_ref.dtype)


def _inproj(x2, g, w):
    n, d = x2.shape
    n_out = w.shape[1]
    return pl.pallas_call(
        _inproj_kernel,
        grid=(n // PROJ_TM,),
        in_specs=[
            pl.BlockSpec((PROJ_TM, d), lambda i: (i, 0)),
            pl.BlockSpec((1, d), lambda i: (0, 0)),
            pl.BlockSpec((d, n_out), lambda i: (0, 0)),
        ],
        out_specs=pl.BlockSpec((PROJ_TM, n_out), lambda i: (i, 0)),
        out_shape=jax.ShapeDtypeStruct((n, n_out), PROJ_DTYPE),
        compiler_params=_params(("parallel",)),
        name="inproj",
    )(x2, g.reshape(1, d), w)


def _outproj_kernel(*refs, n_parts, final):
    x_ref = refs[0]
    parts = refs[1:1 + n_parts]
    w_ref = refs[1 + n_parts]
    o_ref = refs[-1]
    acc = x_ref[...]
    off = 0
    for p in parts:
        wp = p.shape[1]
        acc = acc + _dot(p[...], w_ref[off:off + wp, :])
        off += wp
    if final:
        gf_ref = refs[2 + n_parts]
        ms = jnp.mean(acc * acc, axis=-1, keepdims=True)
        acc = acc * lax.rsqrt(ms + EPS) * gf_ref[...]
    o_ref[...] = acc


def _outproj(x2, parts, w, g_final=None):
    n, d = x2.shape
    final = g_final is not None
    in_specs = [pl.BlockSpec((PROJ_TM, d), lambda i: (i, 0))]
    in_specs += [pl.BlockSpec((PROJ_TM, p.shape[1]), lambda i: (i, 0)) for p in parts]
    in_specs += [pl.BlockSpec(w.shape, lambda i: (0, 0))]
    args = [x2, *parts, w]
    if final:
        in_specs += [pl.BlockSpec((1, d), lambda i: (0, 0))]
        args += [g_final.reshape(1, d)]
    return pl.pallas_call(
        functools.partial(_outproj_kernel, n_parts=len(parts), final=final),
        grid=(n // PROJ_TM,),
        in_specs=in_specs,
        out_specs=pl.BlockSpec((PROJ_TM, d), lambda i: (i, 0)),
        out_shape=jax.ShapeDtypeStruct((n, d), F32),
        compiler_params=_params(("parallel",)),
        name="outproj",
    )(*args)


def _memkv_kernel(m_ref, g_ref, w_ref, o_ref):
    x = m_ref[0]
    ms = jnp.mean(x * x, axis=-1, keepdims=True)
    h = (x * lax.rsqrt(ms + EPS) * g_ref[...]).astype(BF16)
    o_ref[0, 0] = _dot(h, w_ref[0]).astype(o_ref.dtype)


def _memkv(mem, g, w_kv):
    b, nm, d = mem.shape
    depth, _, wkv = w_kv.shape
    return pl.pallas_call(
        _memkv_kernel,
        grid=(depth, b),
        in_specs=[
            pl.BlockSpec((1, nm, d), lambda l, i: (i, 0, 0)),
            pl.BlockSpec((1, d), lambda l, i: (0, 0)),
            pl.BlockSpec((1, d, wkv), lambda l, i: (l, 0, 0)),
        ],
        out_specs=pl.BlockSpec((1, 1, nm, wkv), lambda l, i: (l, i, 0, 0)),
        out_shape=jax.ShapeDtypeStruct((depth, b, nm, wkv), BF16),
        compiler_params=_params(("parallel", "parallel")),
        name="memkv",
    )(mem, g.reshape(1, d), w_kv)


def _memattn_kernel(q_ref, g_ref, kv_ref, o_ref):
    q = q_ref[0].astype(F32) * (HEAD_DIM_M ** -0.5)
    outs = []
    for h in range(N_HEADS_M):
        sl = slice(h * HEAD_DIM_M, (h + 1) * HEAD_DIM_M)
        qh = q[:, sl].astype(BF16)
        kh = kv_ref[0, 0, :, h * HEAD_DIM_M:(h + 1) * HEAD_DIM_M]
        vh = kv_ref[0, 0, :, W_M + h * HEAD_DIM_M:W_M + (h + 1) * HEAD_DIM_M]
        s = _dot_nt(qh, kh)
        m = jnp.max(s, axis=-1, keepdims=True)
        p = jnp.exp(s - m)
        l = jnp.sum(p, axis=-1, keepdims=True)
        outs.append(_dot(p.astype(BF16), vh) / l)
    o = jnp.concatenate(outs, axis=-1)
    o_ref[0] = (o * _silu(g_ref[0].astype(F32))).astype(o_ref.dtype)


def _memattn(proj, kv, layer, q_off, g_off):
    b, t, _ = proj.shape
    qb, gb = q_off // W_M, g_off // W_M
    return pl.pallas_call(
        _memattn_kernel,
        grid=(b, t // MEM_TQ),
        in_specs=[
            pl.BlockSpec((1, MEM_TQ, W_M), lambda i, j: (i, j, qb)),
            pl.BlockSpec((1, MEM_TQ, W_M), lambda i, j: (i, j, gb)),
            pl.BlockSpec((1, 1, N_MEM, 2 * W_M), lambda i, j: (layer, i, 0, 0)),
        ],
        out_specs=pl.BlockSpec((1, MEM_TQ, W_M), lambda i, j: (i, j, 0)),
        out_shape=jax.ShapeDtypeStruct((b, t, W_M), BF16),
        compiler_params=_params(("parallel", "parallel")),
        name="memattn",
    )(proj, proj, kv)


def _winattn_kernel(sink_ref, q_ref, g_ref, k_ref, v_ref, o_ref, *, seq):
    c = pl.program_id(1)
    nb = seq // BLOCK
    q = q_ref[0].astype(F32) * (HEAD_DIM_A ** -0.5)
    ri = lax.broadcasted_iota(jnp.int32, (BLOCK, BLOCK), 0)
    ci = lax.broadcasted_iota(jnp.int32, (BLOCK, BLOCK), 1)
    blocks = []
    for off in (-1, 0, 1):
        dist_i = jnp.abs(ri - ci - off * BLOCK)
        kpos = (c + off) * BLOCK + ci
        valid = (dist_i <= WINDOW) & (kpos >= 0) & (kpos < seq)
        start = pl.multiple_of(jnp.clip(c + off, 0, nb - 1) * BLOCK, BLOCK)
        blocks.append((dist_i.astype(F32), valid, start))
    group = N_Q_A // N_KV_A
    outs = []
    for h in range(N_Q_A):
        n = h // group
        slope = 2.0 ** (-8.0 * (h + 1) / N_Q_A)
        sink = sink_ref[h]
        qh = q[:, h * HEAD_DIM_A:(h + 1) * HEAD_DIM_A].astype(BF16)
        ss = []
        for dist, valid, start in blocks:
            kb = k_ref[0, pl.ds(start, BLOCK), n * HEAD_DIM_A:(n + 1) * HEAD_DIM_A].astype(BF16)
            s = _dot_nt(qh, kb)
            ss.append(jnp.where(valid, s - slope * dist, MASK_VALUE))
        m = jnp.maximum(jnp.maximum(ss[0], ss[1]), ss[2])
        m = jnp.maximum(jnp.max(m, axis=-1, keepdims=True), sink)
        acc = jnp.zeros((BLOCK, HEAD_DIM_A), F32)
        psum = jnp.zeros((BLOCK, BLOCK), F32)
        for s, (dist, valid, start) in zip(ss, blocks):
            p = jnp.exp(s - m)
            psum = psum + p
            vb = v_ref[0, pl.ds(start, BLOCK), n * HEAD_DIM_A:(n + 1) * HEAD_DIM_A].astype(BF16)
            acc = acc + _dot(p.astype(BF16), vb)
        denom = jnp.sum(psum, axis=-1, keepdims=True) + jnp.exp(sink - m)
        outs.append(acc / denom)
    o = jnp.concatenate(outs, axis=-1)
    o_ref[0] = (o * _silu(g_ref[0].astype(F32))).astype(o_ref.dtype)


def _winattn(proj, sink):
    b, t, _ = proj.shape
    return pl.pallas_call(
        functools.partial(_winattn_kernel, seq=t),
        grid=(b, t // BLOCK),
        in_specs=[
            pl.BlockSpec(memory_space=pltpu.SMEM),
            pl.BlockSpec((1, BLOCK, W_A), lambda i, j: (i, j, E_QA // W_A)),
            pl.BlockSpec((1, BLOCK, W_A), lambda i, j: (i, j, E_GA // W_A)),
            pl.BlockSpec((1, t, W_KV_A), lambda i, j: (i, 0, E_KA // W_KV_A)),
            pl.BlockSpec((1, t, W_KV_A), lambda i, j: (i, 0, E_VA // W_KV_A)),
        ],
        out_specs=pl.BlockSpec((1, BLOCK, W_A), lambda i, j: (i, j, 0)),
        out_shape=jax.ShapeDtypeStruct((b, t, W_A), BF16),
        compiler_params=_params(("parallel", "arbitrary")),
        name="winattn",
    )(sink, proj, proj, proj, proj)


def _scan_constants(rev):
    c = CHUNK
    t = np.arange(c)[:, None]
    u = np.arange(c)[None, :]
    mats = []
    mats.append((u >= t) if rev else (u <= t))
    masks = [t == u]
    for lvl in range(1, N_LEVELS + 1):
        blk, half = 2 ** lvl, 2 ** (lvl - 1)
        m = (t // blk) * blk + half
        if rev:
            mat = np.where(t < m, (u >= t) & (u < m), (u >= m) & (u < t))
            mask = (t // blk == u // blk) & (t % blk < half) & (u % blk >= half)
        else:
            mat = np.where(t >= m, (u >= m) & (u <= t), (u > t) & (u < m))
            mask = (t // blk == u // blk) & (t % blk >= half) & (u % blk < half)
        mats.append(mat)
        masks.append(mask)
    mats.append((u < t) if rev else (u > t))
    mall = np.concatenate(mats, axis=0).astype(np.float32)
    return jnp.asarray(mall, dtype=BF16), jnp.asarray(np.stack(masks).astype(np.float32))


def _chunk_step(q, k, v, g, mall_ref, mask_ref, st_ref, rev):
    c = CHUNK
    g1 = g.astype(BF16)
    rem = g - g1.astype(F32)
    g2 = rem.astype(BF16)
    g3 = (rem - g2.astype(F32)).astype(BF16)
    mall = mall_ref[...]
    e_all = jnp.exp(_dot(mall, g1) + _dot(mall, g2) + _dot(mall, g3))
    e_b = e_all[0:c]
    e_tail = e_all[(N_LEVELS + 1) * c:(N_LEVELS + 2) * c]
    e_tot = e_b[0:1] if rev else e_b[c - 1:c]
    vb = v.astype(BF16)
    st = st_ref[...]
    o = _dot_nt((q * e_b).astype(BF16), st.astype(BF16))
    a = mask_ref[0] * _dot_nt(q.astype(BF16), k.astype(BF16))
    for lvl in range(1, N_LEVELS + 1):
        e_l = e_all[lvl * c:(lvl + 1) * c]
        a = a + mask_ref[lvl] * _dot_nt((q * e_l).astype(BF16), (k * e_l).astype(BF16))
    o = o + _dot(a.astype(BF16), vb)
    st_ref[...] = st * e_tot + _dot_tn(vb, (k * e_tail).astype(BF16))
    return o


def _group_norm_gate(o, norm_g, gate):
    ms = jnp.mean(o * o, axis=-1, keepdims=True)
    return o * lax.rsqrt(ms + EPS) * norm_g * _silu(gate)


def _scan_loop(prep_fwd, prep_bwd, finish, refs, seq):
    mallf_ref, maskf_ref, mallb_ref, maskb_ref, oacc_ref, st_ref = refs
    nch = seq // CHUNK

    st_ref[...] = jnp.zeros_like(st_ref)

    def fwd_body(n, carry):
        r0 = pl.multiple_of(n * CHUNK, CHUNK)
        q, k, v, g = prep_fwd(r0)
        oacc_ref[pl.ds(r0, CHUNK), :] = _chunk_step(q, k, v, g, mallf_ref, maskf_ref, st_ref, False)
        return carry

    lax.fori_loop(0, nch, fwd_body, 0)
    st_ref[...] = jnp.zeros_like(st_ref)

    def bwd_body(n, carry):
        r0 = pl.multiple_of((nch - 1 - n) * CHUNK, CHUNK)
        q, k, v, g = prep_bwd(r0)
        o = _chunk_step(q, k, v, g, mallb_ref, maskb_ref, st_ref, True)
        finish(r0, oacc_ref[pl.ds(r0, CHUNK), :] + o)
        return carry

    lax.fori_loop(0, nch, bwd_body, 0)


def _hgrn_kernel(q_ref, zf_ref, zb_ref, i_ref, gate_ref, lb_ref, ng_ref,
                 mallf_ref, maskf_ref, mallb_ref, maskb_ref, o_ref, oacc_ref, st_ref,
                 *, layer_idx, n_even, seq):
    lbp = lb_ref[...].astype(F32)
    rows = [[lbp[l * 2 + d:l * 2 + d + 1] for l in range(n_even)] for d in range(2)]
    lower = []
    for d in range(2):
        mx = functools.reduce(jnp.maximum, rows[d])
        ex = [jnp.exp(r - mx) for r in rows[d]]
        tot = functools.reduce(lambda a, b: a + b, ex)
        lbs = [e / tot for e in ex]
        cum = functools.reduce(lambda a, b: a + b, lbs[:layer_idx + 1])
        lower.append(cum - lbs[0])

    def prep(r0, z_ref, lb):
        rs = pl.ds(r0, CHUNK)
        sg = jax.nn.sigmoid(z_ref[0, rs, :].astype(F32))
        f = lb + (1.0 - lb) * sg
        g = jnp.log(jnp.maximum(f, MIN_GATE))
        k = (1.0 - lb) * (1.0 - sg)
        q = _silu(q_ref[0, rs, :].astype(F32))
        return q, k, i_ref[0, rs, :].astype(F32), g

    def finish(r0, o):
        rs = pl.ds(r0, CHUNK)
        y = _group_norm_gate(o, ng_ref[...], gate_ref[0, rs, :].astype(F32))
        o_ref[0, rs, :] = y.astype(o_ref.dtype)

    _scan_loop(lambda r0: prep(r0, zf_ref, lower[0]), lambda r0: prep(r0, zb_ref, lower[1]),
               finish, (mallf_ref, maskf_ref, mallb_ref, maskb_ref, oacc_ref, st_ref), seq)


def _const_spec(a):
    nd = a.ndim
    return pl.BlockSpec(a.shape, lambda i, j: (0,) * nd)


def _hgrn(proj, lb_param, norm_g, layer_idx, consts):
    b, t, _ = proj.shape
    n_even = lb_param.shape[0]
    hd = HEAD_DIM_B
    col = lambda off: pl.BlockSpec((1, t, hd), lambda i, j: (i, 0, off // hd + j))
    return pl.pallas_call(
        functools.partial(_hgrn_kernel, layer_idx=layer_idx, n_even=n_even, seq=t),
        grid=(b, N_HEADS_B),
        in_specs=[col(E_QB), col(E_ZF), col(E_ZB), col(E_IB), col(E_GB),
                  pl.BlockSpec((n_even * 2, hd), lambda i, j: (0, j)),
                  pl.BlockSpec((1, hd), lambda i, j: (0, j)),
                  *[_const_spec(a) for a in consts]],
        out_specs=pl.BlockSpec((1, t, hd), lambda i, j: (i, 0, j)),
        out_shape=jax.ShapeDtypeStruct((b, t, W_B), BF16),
        scratch_shapes=[pltpu.VMEM((t, hd), F32), pltpu.VMEM((hd, hd), F32)],
        compiler_params=_params(("parallel", "parallel")),
        name="hgrn",
    )(proj, proj, proj, proj, proj, lb_param.reshape(n_even * 2, W_B),
      norm_g.reshape(1, W_B), *consts)


def _gla_kernel(q_ref, k_ref, v_ref, gate_ref, r_ref, wuf_ref, wub_ref, bf_ref, bb_ref, ng_ref,
                mallf_ref, maskf_ref, mallb_ref, maskb_ref, o_ref, oacc_ref, st_ref, *, seq):
    def prep(r0, wu_ref, bias_ref):
        rs = pl.ds(r0, CHUNK)
        r = r_ref[0, rs, :].astype(F32)
        x = jnp.dot(r, wu_ref[...], preferred_element_type=F32,
                    precision=lax.Precision.HIGHEST) + bias_ref[...]
        g = (jnp.minimum(x, 0.0) - jnp.log(1.0 + jnp.exp(-jnp.abs(x)))) * (1.0 / GATE_TEMP)
        q = q_ref[0, rs, :].astype(F32) * (DK_C ** -0.5)
        return q, k_ref[0, rs, :].astype(F32), v_ref[0, rs, :].astype(F32), g

    def finish(r0, o):
        rs = pl.ds(r0, CHUNK)
        y = _group_norm_gate(o, ng_ref[...], gate_ref[0, rs, :].astype(F32))
        o_ref[0, rs, :] = y.astype(o_ref.dtype)

    _scan_loop(lambda r0: prep(r0, wuf_ref, bf_ref), lambda r0: prep(r0, wub_ref, bb_ref),
               finish, (mallf_ref, maskf_ref, mallb_ref, maskb_ref, oacc_ref, st_ref), seq)


def _gla(proj, w_up, b_gate, norm_g, consts):
    b, t, _ = proj.shape
    dk, dv = DK_C, DV_C
    wuf = jnp.zeros((LANE, WK_C), F32).at[0:GATE_RANK].set(w_up[0])
    wub = jnp.zeros((LANE, WK_C), F32).at[GATE_RANK:2 * GATE_RANK].set(w_up[1])
    colk = lambda off: pl.BlockSpec((1, t, dk), lambda i, j: (i, 0, off // dk + j))
    colv = lambda off: pl.BlockSpec((1, t, dv), lambda i, j: (i, 0, off // dv + j))
    return pl.pallas_call(
        functools.partial(_gla_kernel, seq=t),
        grid=(b, N_HEADS_C),
        in_specs=[colk(O_QC), colk(O_KC), colv(O_VC), colv(O_GC),
                  pl.BlockSpec((1, t, LANE), lambda i, j: (i, 0, O_R // LANE)),
                  pl.BlockSpec((LANE, dk), lambda i, j: (0, j)),
                  pl.BlockSpec((LANE, dk), lambda i, j: (0, j)),
                  pl.BlockSpec((1, dk), lambda i, j: (0, j)),
                  pl.BlockSpec((1, dk), lambda i, j: (0, j)),
                  pl.BlockSpec((1, dv), lambda i, j: (0, j)),
                  *[_const_spec(a) for a in consts]],
        out_specs=pl.BlockSpec((1, t, dv), lambda i, j: (i, 0, j)),
        out_shape=jax.ShapeDtypeStruct((b, t, WV_C), BF16),
        scratch_shapes=[pltpu.VMEM((t, dv), F32), pltpu.VMEM((dv, dk), F32)],
        compiler_params=_params(("parallel", "parallel")),
        name="gla",
    )(proj, proj, proj, proj, proj, wuf, wub, b_gate[0].reshape(1, WK_C),
      b_gate[1].reshape(1, WK_C), norm_g.reshape(1, WV_C), *consts)


def _even_w_in(w):
    w = w.astype(BF16)
    return jnp.concatenate([w[:, 0:512], w[:, 768:4864], w[:, 512:768]], axis=1)


def _odd_w_in(w):
    w = w.astype(BF16)
    pad = jnp.zeros((w.shape[0], ODD_IN_PAD - 4128), BF16)
    return jnp.concatenate([w[:, 0:3072], w[:, 3104:4128], w[:, 3072:3104], pad], axis=1)


def kernel(x, mem, norm_even, w_in_even, sink, lb_param, hgrn_norm, w_out_even, norm_odd,
           w_in_odd, w_gate_up, b_gate, gla_norm, w_out_odd, mem_norm, w_mem_kv, final_norm):
    b, t, d = x.shape
    consts = (*_scan_constants(False), *_scan_constants(True))
    kv_all = _memkv(mem, mem_norm, w_mem_kv.astype(BF16))
    x2 = x.reshape(b * t, d)
    for l in range(DEPTH):
        i = l // 2
        g_final = final_norm if l == DEPTH - 1 else None
        if l % 2 == 0:
            proj = _inproj(x2, norm_even[i], _even_w_in(w_in_even[i])).reshape(b, t, EVEN_IN)
            a = _winattn(proj, sink[i])
            o = _hgrn(proj, lb_param, hgrn_norm[i], i, consts)
            mo = _memattn(proj, kv_all, l, E_QM, E_GM)
            parts = [a.reshape(b * t, W_A), o.reshape(b * t, W_B), mo.reshape(b * t, W_M)]
            x2 = _outproj(x2, parts, w_out_even[i].astype(BF16), g_final)
        else:
            proj = _inproj(x2, norm_odd[i], _odd_w_in(w_in_odd[i])).reshape(b, t, ODD_IN_PAD)
            o = _gla(proj, w_gate_up[i], b_gate[i], gla_norm[i], consts)
            mo = _memattn(proj, kv_all, l, O_QM, O_GM)
            parts = [o.reshape(b * t, WV_C), mo.reshape(b * t, W_M)]
            x2 = _outproj(x2, parts, w_out_odd[i].astype(BF16), g_final)
    return x2.reshape(b, t, d)
```

```python
import functools

import numpy as np
import jax
import jax.numpy as jnp
from jax import lax
from jax.experimental import pallas as pl
from jax.experimental.pallas import tpu as pltpu

F32 = jnp.float32
BF16 = jnp.bfloat16

D_MODEL = 1024
DEPTH = 4
N_Q_A, N_KV_A, HEAD_DIM_A = 8, 2, 64
W_A, W_KV_A = 512, 128
WINDOW = 128
BLOCK = 128
N_HEADS_B, HEAD_DIM_B, W_B = 4, 128, 512
N_HEADS_C, DK_C, DV_C = 4, 128, 256
WK_C, WV_C = 512, 1024
GATE_RANK = 16
GATE_TEMP = 16.0
N_MEM, N_HEADS_M, HEAD_DIM_M, W_M = 256, 4, 128, 512
EPS = 1e-6
MASK_VALUE = -1e30
MIN_GATE = 1e-30
MIX = 1536

EVEN_IN = 4864
E_QA, E_GA, E_QB, E_ZF, E_ZB, E_IB, E_GB, E_QM, E_GM, E_KA, E_VA = (
    0, 512, 1024, 1536, 2048, 2560, 3072, 3584, 4096, 4608, 4736)
ODD_IN_PAD = 4352
O_QC, O_KC, O_VC, O_GC, O_QM, O_GM, O_R = 0, 512, 1024, 2048, 3072, 3584, 4096

LANE = 128
PROJ_DTYPE = F32
PROJ_TM = 256
PROJ_TN = 256
CHUNK = 64
N_LEVELS = 6
SCAN_UNROLL = 2
FIN_ROWS = 256
MEM_TQ = 512
VMEM_LIMIT = 56 * 1024 * 1024

_NT = (((1,), (1,)), ((), ()))
_TN = (((0,), (0,)), ((), ()))


def _dot(a, b):
    return jnp.dot(a, b, preferred_element_type=F32)


def _dot_nt(a, b):
    return lax.dot_general(a, b, _NT, preferred_element_type=F32)


def _dot_tn(a, b):
    return lax.dot_general(a, b, _TN, preferred_element_type=F32)


def _silu(x):
    return x * jax.nn.sigmoid(x)


def _params(sem):
    return pltpu.CompilerParams(dimension_semantics=sem, vmem_limit_bytes=VMEM_LIMIT)


def _inproj_kernel(x_ref, g_ref, w_ref, o_ref):
    x = x_ref[...]
    ms = jnp.mean(x * x, axis=-1, keepdims=True)
    h = (x * lax.rsqrt(ms + EPS) * g_ref[...]).astype(BF16)
    n_out = o_ref.shape[1]
    for c in range(0, n_out, PROJ_TN):
        o_ref[:, c:c + PROJ_TN] = _dot(h, w_ref[:, c:c + PROJ_TN]).astype(o_ref.dtype)


def _inproj(x2, g, w):
    n, d = x2.shape
    n_out = w.shape[1]
    return pl.pallas_call(
        _inproj_kernel,
        grid=(n // PROJ_TM,),
        in_specs=[
            pl.BlockSpec((PROJ_TM, d), lambda i: (i, 0)),
            pl.BlockSpec((1, d), lambda i: (0, 0)),
            pl.BlockSpec((d, n_out), lambda i: (0, 0)),
        ],
        out_specs=pl.BlockSpec((PROJ_TM, n_out), lambda i: (i, 0)),
        out_shape=jax.ShapeDtypeStruct((n, n_out), PROJ_DTYPE),
        compiler_params=_params(("parallel",)),
        name="inproj",
    )(x2, g.reshape(1, d), w)


def _outproj_kernel(*refs, n_parts, final):
    x_ref = refs[0]
    parts = refs[1:1 + n_parts]
    w_ref = refs[1 + n_parts]
    o_ref = refs[-1]
    acc = x_ref[...]
    off = 0
    for p in parts:
        wp = p.shape[1]
        acc = acc + _dot(p[...], w_ref[off:off + wp, :])
        off += wp
    if final:
        gf_ref = refs[2 + n_parts]
        ms = jnp.mean(acc * acc, axis=-1, keepdims=True)
        acc = acc * lax.rsqrt(ms + EPS) * gf_ref[...]
    o_ref[...] = acc


def _outproj(x2, parts, w, g_final=None):
    n, d = x2.shape
    final = g_final is not None
    in_specs = [pl.BlockSpec((PROJ_TM, d), lambda i: (i, 0))]
    in_specs += [pl.BlockSpec((PROJ_TM, p.shape[1]), lambda i: (i, 0)) for p in parts]
    in_specs += [pl.BlockSpec(w.shape, lambda i: (0, 0))]
    args = [x2, *parts, w]
    if final:
        in_specs += [pl.BlockSpec((1, d), lambda i: (0, 0))]
        args += [g_final.reshape(1, d)]
    return pl.pallas_call(
        functools.partial(_outproj_kernel, n_parts=len(parts), final=final),
        grid=(n // PROJ_TM,),
        in_specs=in_specs,
        out_specs=pl.BlockSpec((PROJ_TM, d), lambda i: (i, 0)),
        out_shape=jax.ShapeDtypeStruct((n, d), F32),
        compiler_params=_params(("parallel",)),
        name="outproj",
    )(*args)


def _memkv_kernel(m_ref, g_ref, w_ref, o_ref):
    x = m_ref[0]
    ms = jnp.mean(x * x, axis=-1, keepdims=True)
    h = (x * lax.rsqrt(ms + EPS) * g_ref[...]).astype(BF16)
    o_ref[0, 0] = _dot(h, w_ref[0]).astype(o_ref.dtype)


def _memkv(mem, g, w_kv):
    b, nm, d = mem.shape
    depth, _, wkv = w_kv.shape
    return pl.pallas_call(
        _memkv_kernel,
        grid=(depth, b),
        in_specs=[
            pl.BlockSpec((1, nm, d), lambda l, i: (i, 0, 0)),
            pl.BlockSpec((1, d), lambda l, i: (0, 0)),
            pl.BlockSpec((1, d, wkv), lambda l, i: (l, 0, 0)),
        ],
        out_specs=pl.BlockSpec((1, 1, nm, wkv), lambda l, i: (l, i, 0, 0)),
        out_shape=jax.ShapeDtypeStruct((depth, b, nm, wkv), BF16),
        compiler_params=_params(("parallel", "parallel")),
        name="memkv",
    )(mem, g.reshape(1, d), w_kv)


def _memattn_kernel(q_ref, g_ref, kv_ref, o_ref):
    q = q_ref[0].astype(F32) * (HEAD_DIM_M ** -0.5)
    outs = []
    for h in range(N_HEADS_M):
        sl = slice(h * HEAD_DIM_M, (h + 1) * HEAD_DIM_M)
        qh = q[:, sl].astype(BF16)
        kh = kv_ref[0, 0, :, h * HEAD_DIM_M:(h + 1) * HEAD_DIM_M]
        vh = kv_ref[0, 0, :, W_M + h * HEAD_DIM_M:W_M + (h + 1) * HEAD_DIM_M]
        s = _dot_nt(qh, kh)
        m = jnp.max(s, axis=-1, keepdims=True)
        p = jnp.exp(s - m)
        l = jnp.sum(p, axis=-1, keepdims=True)
        outs.append(_dot(p.astype(BF16), vh) / l)
    o = jnp.concatenate(outs, axis=-1)
    o_ref[0] = (o * _silu(g_ref[0].astype(F32))).astype(o_ref.dtype)


def _memattn(proj, kv, layer, q_off, g_off):
    b, t, _ = proj.shape
    qb, gb = q_off // W_M, g_off // W_M
    return pl.pallas_call(
        _memattn_kernel,
        grid=(b, t // MEM_TQ),
        in_specs=[
            pl.BlockSpec((1, MEM_TQ, W_M), lambda i, j: (i, j, qb)),
            pl.BlockSpec((1, MEM_TQ, W_M), lambda i, j: (i, j, gb)),
            pl.BlockSpec((1, 1, N_MEM, 2 * W_M), lambda i, j: (layer, i, 0, 0)),
        ],
        out_specs=pl.BlockSpec((1, MEM_TQ, W_M), lambda i, j: (i, j, 0)),
        out_shape=jax.ShapeDtypeStruct((b, t, W_M), BF16),
        compiler_params=_params(("parallel", "parallel")),
        name="memattn",
    )(proj, proj, kv)


def _winattn_kernel(sink_ref, q_ref, g_ref, k_ref, v_ref, o_ref, *, seq):
    c = pl.program_id(1)
    nb = seq // BLOCK
    q = q_ref[0].astype(F32) * (HEAD_DIM_A ** -0.5)
    ri = lax.broadcasted_iota(jnp.int32, (BLOCK, BLOCK), 0)
    ci = lax.broadcasted_iota(jnp.int32, (BLOCK, BLOCK), 1)
    blocks = []
    for off in (-1, 0, 1):
        dist_i = jnp.abs(ri - ci - off * BLOCK)
        kpos = (c + off) * BLOCK + ci
        valid = (dist_i <= WINDOW) & (kpos >= 0) & (kpos < seq)
        start = pl.multiple_of(jnp.clip(c + off, 0, nb - 1) * BLOCK, BLOCK)
        blocks.append((dist_i.astype(F32), valid, start))
    group = N_Q_A // N_KV_A
    outs = []
    for h in range(N_Q_A):
        n = h // group
        slope = 2.0 ** (-8.0 * (h + 1) / N_Q_A)
        sink = sink_ref[h]
        qh = q[:, h * HEAD_DIM_A:(h + 1) * HEAD_DIM_A].astype(BF16)
        ss = []
        for dist, valid, start in blocks:
            kb = k_ref[0, pl.ds(start, BLOCK), n * HEAD_DIM_A:(n + 1) * HEAD_DIM_A].astype(BF16)
            s = _dot_nt(qh, kb)
            ss.append(jnp.where(valid, s - slope * dist, MASK_VALUE))
        m = jnp.maximum(jnp.maximum(ss[0], ss[1]), ss[2])
        m = jnp.maximum(jnp.max(m, axis=-1, keepdims=True), sink)
        acc = jnp.zeros((BLOCK, HEAD_DIM_A), F32)
        psum = jnp.zeros((BLOCK, BLOCK), F32)
        for s, (dist, valid, start) in zip(ss, blocks):
            p = jnp.exp(s - m)
            psum = psum + p
            vb = v_ref[0, pl.ds(start, BLOCK), n * HEAD_DIM_A:(n + 1) * HEAD_DIM_A].astype(BF16)
            acc = acc + _dot(p.astype(BF16), vb)
        denom = jnp.sum(psum, axis=-1, keepdims=True) + jnp.exp(sink - m)
        outs.append(acc / denom)
    o = jnp.concatenate(outs, axis=-1)
    o_ref[0] = (o * _silu(g_ref[0].astype(F32))).astype(o_ref.dtype)


def _winattn(proj, sink):
    b, t, _ = proj.shape
    return pl.pallas_call(
        functools.partial(_winattn_kernel, seq=t),
        grid=(b, t // BLOCK),
        in_specs=[
            pl.BlockSpec(memory_space=pltpu.SMEM),
            pl.BlockSpec((1, BLOCK, W_A), lambda i, j: (i, j, E_QA // W_A)),
            pl.BlockSpec((1, BLOCK, W_A), lambda i, j: (i, j, E_GA // W_A)),
            pl.BlockSpec((1, t, W_KV_A), lambda i, j: (i, 0, E_KA // W_KV_A)),
            pl.BlockSpec((1, t, W_KV_A), lambda i, j: (i, 0, E_VA // W_KV_A)),
        ],
        out_specs=pl.BlockSpec((1, BLOCK, W_A), lambda i, j: (i, j, 0)),
        out_shape=jax.ShapeDtypeStruct((b, t, W_A), BF16),
        compiler_params=_params(("parallel", "arbitrary")),
        name="winattn",
    )(sink, proj, proj, proj, proj)


def _scan_constants(rev):
    c = CHUNK
    t = np.arange(c)[:, None]
    u = np.arange(c)[None, :]
    mats = []
    mats.append((u >= t) if rev else (u <= t))
    masks = [t == u]
    for lvl in range(1, N_LEVELS + 1):
        blk, half = 2 ** lvl, 2 ** (lvl - 1)
        m = (t // blk) * blk + half
        if rev:
            mat = np.where(t < m, (u >= t) & (u < m), (u >= m) & (u < t))
            mask = (t // blk == u // blk) & (t % blk < half) & (u % blk >= half)
        else:
            mat = np.where(t >= m, (u >= m) & (u <= t), (u > t) & (u < m))
            mask = (t // blk == u // blk) & (t % blk >= half) & (u % blk < half)
        mats.append(mat)
        masks.append(mask)
    mats.append((u < t) if rev else (u > t))
    mall = np.concatenate(mats, axis=0).astype(np.float32)
    mall3 = np.concatenate([mall, mall, mall], axis=1)
    return jnp.asarray(mall3, dtype=BF16), jnp.asarray(np.stack(masks).astype(np.float32))


def _split3(g):
    g1 = g.astype(BF16)
    rem = g - g1.astype(F32)
    g2 = rem.astype(BF16)
    g3 = (rem - g2.astype(F32)).astype(BF16)
    return jnp.concatenate([g1, g2, g3], axis=0)


def _scan_group(dirs):
    c = CHUNK
    es = []
    for chunks, mall3_ref, _, _, _ in dirs:
        g123 = jnp.concatenate([_split3(g) for (_, _, _, g) in chunks], axis=1)
        es.append(jnp.exp(_dot(mall3_ref[...], g123)))

    def e_rows(di, u, blk):
        dk = dirs[di][0][u][0].shape[1]
        return es[di][blk * c:(blk + 1) * c, u * dk:(u + 1) * dk]

    chains = [(di, u) for u in range(SCAN_UNROLL) for di in range(len(dirs))]
    a = {}
    for lvl in range(N_LEVELS + 1):
        for di, u in chains:
            q, k, _, _ = dirs[di][0][u]
            if lvl == 0:
                p = _dot_nt(q.astype(BF16), k.astype(BF16))
            else:
                e_l = e_rows(di, u, lvl)
                p = _dot_nt((q * e_l).astype(BF16), (k * e_l).astype(BF16))
            p = dirs[di][2][lvl] * p
            a[di, u] = p if lvl == 0 else a[di, u] + p
    sts = [d[3][...] for d in dirs]
    outs = [[None] * SCAN_UNROLL for _ in dirs]
    for u in range(SCAN_UNROLL):
        for di, (chunks, _, _, _, rev) in enumerate(dirs):
            q, k, v, _ = chunks[u]
            e_b = e_rows(di, u, 0)
            e_tail = e_rows(di, u, N_LEVELS + 1)
            e_tot = e_b[0:1] if rev else e_b[c - 1:c]
            vb = v.astype(BF16)
            st = sts[di]
            outs[di][u] = (_dot_nt((q * e_b).astype(BF16), st.astype(BF16))
                           + _dot(a[di, u].astype(BF16), vb))
            sts[di] = st * e_tot + _dot_tn(vb, (k * e_tail).astype(BF16))
    for di, d in enumerate(dirs):
        d[3][...] = sts[di]
    return outs


def _group_norm_gate(o, norm_g, gate):
    ms = jnp.mean(o * o, axis=-1, keepdims=True)
    return o * lax.rsqrt(ms + EPS) * norm_g * _silu(gate)


def _scan_loop(prep_fwd, prep_bwd, finish, refs, seq):
    mallf_ref, maskf_ref, mallb_ref, maskb_ref, oaccf_ref, oaccb_ref, stf_ref, stb_ref = refs
    nch = seq // CHUNK
    stf_ref[...] = jnp.zeros_like(stf_ref)
    stb_ref[...] = jnp.zeros_like(stb_ref)

    def body(n, carry):
        rows_f = [pl.ds(pl.multiple_of((n * SCAN_UNROLL + u) * CHUNK, CHUNK), CHUNK)
                  for u in range(SCAN_UNROLL)]
        rows_b = [pl.ds(pl.multiple_of((nch - 1 - n * SCAN_UNROLL - u) * CHUNK, CHUNK), CHUNK)
                  for u in range(SCAN_UNROLL)]
        outs = _scan_group([
            ([prep_fwd(rs) for rs in rows_f], mallf_ref, maskf_ref, stf_ref, False),
            ([prep_bwd(rs) for rs in rows_b], mallb_ref, maskb_ref, stb_ref, True)])
        for u in range(SCAN_UNROLL):
            oaccf_ref[rows_f[u], :] = outs[0][u]
            oaccb_ref[rows_b[u], :] = outs[1][u]
        return carry

    lax.fori_loop(0, nch // SCAN_UNROLL, body, 0)

    def fin_body(n, carry):
        rs = pl.ds(pl.multiple_of(n * FIN_ROWS, FIN_ROWS), FIN_ROWS)
        finish(rs, oaccf_ref[rs, :] + oaccb_ref[rs, :])
        return carry

    lax.fori_loop(0, seq // FIN_ROWS, fin_body, 0)


def _hgrn_kernel(q_ref, zf_ref, zb_ref, i_ref, gate_ref, lb_ref, ng_ref,
                 mallf_ref, maskf_ref, mallb_ref, maskb_ref, o_ref, *scratch,
                 layer_idx, n_even, seq):
    lbp = lb_ref[...].astype(F32)
    rows = [[lbp[l * 2 + d:l * 2 + d + 1] for l in range(n_even)] for d in range(2)]
    lower = []
    for d in range(2):
        mx = functools.reduce(jnp.maximum, rows[d])
        ex = [jnp.exp(r - mx) for r in rows[d]]
        tot = functools.reduce(lambda a, b: a + b, ex)
        lbs = [e / tot for e in ex]
        cum = functools.reduce(lambda a, b: a + b, lbs[:layer_idx + 1])
        lower.append(cum - lbs[0])

    def prep(rs, z_ref, lb):
        sg = jax.nn.sigmoid(z_ref[0, rs, :].astype(F32))
        f = lb + (1.0 - lb) * sg
        g = jnp.log(jnp.maximum(f, MIN_GATE))
        k = (1.0 - lb) * (1.0 - sg)
        q = _silu(q_ref[0, rs, :].astype(F32))
        return q, k, i_ref[0, rs, :].astype(F32), g

    def finish(rs, o):
        y = _group_norm_gate(o, ng_ref[...], gate_ref[0, rs, :].astype(F32))
        o_ref[0, rs, :] = y.astype(o_ref.dtype)

    _scan_loop(lambda rs: prep(rs, zf_ref, lower[0]), lambda rs: prep(rs, zb_ref, lower[1]),
               finish, (mallf_ref, maskf_ref, mallb_ref, maskb_ref, *scratch), seq)


def _const_spec(a):
    nd = a.ndim
    return pl.BlockSpec(a.shape, lambda i, j: (0,) * nd)


def _hgrn(proj, lb_param, norm_g, layer_idx, consts):
    b, t, _ = proj.shape
    n_even = lb_param.shape[0]
    hd = HEAD_DIM_B
    col = lambda off: pl.BlockSpec((1, t, hd), lambda i, j: (i, 0, off // hd + j))
    return pl.pallas_call(
        functools.partial(_hgrn_kernel, layer_idx=layer_idx, n_even=n_even, seq=t),
        grid=(b, N_HEADS_B),
        in_specs=[col(E_QB), col(E_ZF), col(E_ZB), col(E_IB), col(E_GB),
                  pl.BlockSpec((n_even * 2, hd), lambda i, j: (0, j)),
                  pl.BlockSpec((1, hd), lambda i, j: (0, j)),
                  *[_const_spec(a) for a in consts]],
        out_specs=pl.BlockSpec((1, t, hd), lambda i, j: (i, 0, j)),
        out_shape=jax.ShapeDtypeStruct((b, t, W_B), BF16),
        scratch_shapes=[pltpu.VMEM((t, hd), F32), pltpu.VMEM((t, hd), F32),
                        pltpu.VMEM((hd, hd), F32), pltpu.VMEM((hd, hd), F32)],
        compiler_params=_params(("parallel", "parallel")),
        name="hgrn",
    )(proj, proj, proj, proj, proj, lb_param.reshape(n_even * 2, W_B),
      norm_g.reshape(1, W_B), *consts)


def _gla_kernel(q_ref, k_ref, v_ref, gate_ref, r_ref, wuf_ref, wub_ref, bf_ref, bb_ref, ng_ref,
                mallf_ref, maskf_ref, mallb_ref, maskb_ref, o_ref, *scratch, seq):
    def prep(rs, wu_ref, bias_ref):
        r = r_ref[0, rs, :].astype(F32)
        x = jnp.dot(r, wu_ref[...], preferred_element_type=F32,
                    precision=lax.Precision.HIGHEST) + bias_ref[...]
        g = (jnp.minimum(x, 0.0) - jnp.log(1.0 + jnp.exp(-jnp.abs(x)))) * (1.0 / GATE_TEMP)
        q = q_ref[0, rs, :].astype(F32) * (DK_C ** -0.5)
        return q, k_ref[0, rs, :].astype(F32), v_ref[0, rs, :].astype(F32), g

    def finish(rs, o):
        y = _group_norm_gate(o, ng_ref[...], gate_ref[0, rs, :].astype(F32))
        o_ref[0, rs, :] = y.astype(o_ref.dtype)

    _scan_loop(lambda rs: prep(rs, wuf_ref, bf_ref), lambda rs: prep(rs, wub_ref, bb_ref),
               finish, (mallf_ref, maskf_ref, mallb_ref, maskb_ref, *scratch), seq)


def _gla(proj, w_up, b_gate, norm_g, consts):
    b, t, _ = proj.shape
    dk, dv = DK_C, DV_C
    wuf = jnp.zeros((LANE, WK_C), F32).at[0:GATE_RANK].set(w_up[0])
    wub = jnp.zeros((LANE, WK_C), F32).at[GATE_RANK:2 * GATE_RANK].set(w_up[1])
    colk = lambda off: pl.BlockSpec((1, t, dk), lambda i, j: (i, 0, off // dk + j))
    colv = lambda off: pl.BlockSpec((1, t, dv), lambda i, j: (i, 0, off // dv + j))
    return pl.pallas_call(
        functools.partial(_gla_kernel, seq=t),
        grid=(b, N_HEADS_C),
        in_specs=[colk(O_QC), colk(O_KC), colv(O_VC), colv(O_GC),
                  pl.BlockSpec((1, t, LANE), lambda i, j: (i, 0, O_R // LANE)),
                  pl.BlockSpec((LANE, dk), lambda i, j: (0, j)),
                  pl.BlockSpec((LANE, dk), lambda i, j: (0, j)),
                  pl.BlockSpec((1, dk), lambda i, j: (0, j)),
                  pl.BlockSpec((1, dk), lambda i, j: (0, j)),
                  pl.BlockSpec((1, dv), lambda i, j: (0, j)),
                  *[_const_spec(a) for a in consts]],
        out_specs=pl.BlockSpec((1, t, dv), lambda i, j: (i, 0, j)),
        out_shape=jax.ShapeDtypeStruct((b, t, WV_C), BF16),
        scratch_shapes=[pltpu.VMEM((t, dv), F32), pltpu.VMEM((t, dv), F32),
                        pltpu.VMEM((dv, dk), F32), pltpu.VMEM((dv, dk), F32)],
        compiler_params=_params(("parallel", "parallel")),
        name="gla",
    )(proj, proj, proj, proj, proj, wuf, wub, b_gate[0].reshape(1, WK_C),
      b_gate[1].reshape(1, WK_C), norm_g.reshape(1, WV_C), *consts)


def _even_w_in(w):
    w = w.astype(BF16)
    return jnp.concatenate([w[:, 0:512], w[:, 768:4864], w[:, 512:768]], axis=1)


def _odd_w_in(w):
    w = w.astype(BF16)
    pad = jnp.zeros((w.shape[0], ODD_IN_PAD - 4128), BF16)
    return jnp.concatenate([w[:, 0:3072], w[:, 3104:4128], w[:, 3072:3104], pad], axis=1)


def kernel(x, mem, norm_even, w_in_even, sink, lb_param, hgrn_norm, w_out_even, norm_odd,
           w_in_odd, w_gate_up, b_gate, gla_norm, w_out_odd, mem_norm, w_mem_kv, final_norm):
    b, t, d = x.shape
    consts = (*_scan_constants(False), *_scan_constants(True))
    kv_all = _memkv(mem, mem_norm, w_mem_kv.astype(BF16))
    x2 = x.reshape(b * t, d)
    for l in range(DEPTH):
        i = l // 2
        g_final = final_norm if l == DEPTH - 1 else None
        if l % 2 == 0:
            proj = _inproj(x2, norm_even[i], _even_w_in(w_in_even[i])).reshape(b, t, EVEN_IN)
            a = _winattn(proj, sink[i])
            o = _hgrn(proj, lb_param, hgrn_norm[i], i, consts)
            mo = _memattn(proj, kv_all, l, E_QM, E_GM)
            parts = [a.reshape(b * t, W_A), o.reshape(b * t, W_B), mo.reshape(b * t, W_M)]
            x2 = _outproj(x2, parts, w_out_even[i].astype(BF16), g_final)
        else:
            proj = _inproj(x2, norm_odd[i], _odd_w_in(w_in_odd[i])).reshape(b, t, ODD_IN_PAD)
            o = _gla(proj, w_gate_up[i], b_gate[i], gla_norm[i], consts)
            mo = _memattn(proj, kv_all, l, O_QM, O_GM)
            parts = [o.reshape(b * t, WV_C), mo.reshape(b * t, W_M)]
            x2 = _outproj(x2, parts, w_out_odd[i].astype(BF16), g_final)
    return x2.reshape(b, t, d)
```

```python
import functools

import numpy as np
import jax
import jax.numpy as jnp
from jax import lax
from jax.experimental import pallas as pl
from jax.experimental.pallas import tpu as pltpu

F32 = jnp.float32
BF16 = jnp.bfloat16

D_MODEL = 1024
DEPTH = 4
N_Q_A, N_KV_A, HEAD_DIM_A = 8, 2, 64
W_A, W_KV_A = 512, 128
WINDOW = 128
BLOCK = 128
N_HEADS_B, HEAD_DIM_B, W_B = 4, 128, 512
N_HEADS_C, DK_C, DV_C = 4, 128, 256
WK_C, WV_C = 512, 1024
GATE_RANK = 16
GATE_TEMP = 16.0
N_MEM, N_HEADS_M, HEAD_DIM_M, W_M = 256, 4, 128, 512
EPS = 1e-6
MASK_VALUE = -1e30
MIN_GATE = 1e-30
LOG2E = 1.4426950408889634

EVEN_IN = 4864
E_QA, E_GA, E_QB, E_ZF, E_ZB, E_IB, E_GB, E_QM, E_GM, E_KA, E_VA = (
    0, 512, 1024, 1536, 2048, 2560, 3072, 3584, 4096, 4608, 4736)
ODD_IN_PAD = 4352
O_QC, O_KC, O_VC, O_GC, O_QM, O_GM, O_R = 0, 512, 1024, 2048, 3072, 3584, 4096

LANE = 128
PROJ_DTYPE = F32
PROJ_TM = 256
PROJ_TN = 256
CHUNK = 64
N_LEVELS = 6
SCAN_UNROLL = 4
FIN_ROWS = 256
MEM_TQ = 512
VMEM_LIMIT = 56 * 1024 * 1024

_NT = (((1,), (1,)), ((), ()))
_TN = (((0,), (0,)), ((), ()))


def _dot(a, b):
    return jnp.dot(a, b, preferred_element_type=F32)


def _dot_nt(a, b):
    return lax.dot_general(a, b, _NT, preferred_element_type=F32)


def _dot_tn(a, b):
    return lax.dot_general(a, b, _TN, preferred_element_type=F32)


def _silu(x):
    return x * jax.nn.sigmoid(x)


def _params(sem):
    return pltpu.CompilerParams(dimension_semantics=sem, vmem_limit_bytes=VMEM_LIMIT)


def _inproj_kernel(x_ref, g_ref, w_ref, o_ref):
    x = x_ref[...]
    ms = jnp.mean(x * x, axis=-1, keepdims=True)
    h = (x * lax.rsqrt(ms + EPS) * g_ref[...]).astype(BF16)
    n_out = o_ref.shape[1]
    for c in range(0, n_out, PROJ_TN):
        o_ref[:, c:c + PROJ_TN] = _dot(h, w_ref[:, c:c + PROJ_TN]).astype(o_ref.dtype)


def _inproj(x2, g, w):
    n, d = x2.shape
    n_out = w.shape[1]
    return pl.pallas_call(
        _inproj_kernel,
        grid=(n // PROJ_TM,),
        in_specs=[
            pl.BlockSpec((PROJ_TM, d), lambda i: (i, 0)),
            pl.BlockSpec((1, d), lambda i: (0, 0)),
            pl.BlockSpec((d, n_out), lambda i: (0, 0)),
        ],
        out_specs=pl.BlockSpec((PROJ_TM, n_out), lambda i: (i, 0)),
        out_shape=jax.ShapeDtypeStruct((n, n_out), PROJ_DTYPE),
        compiler_params=_params(("parallel",)),
        name="inproj",
    )(x2, g.reshape(1, d), w)


def _outproj_kernel(*refs, n_parts, final):
    x_ref = refs[0]
    parts = refs[1:1 + n_parts]
    w_ref = refs[1 + n_parts]
    o_ref = refs[-1]
    acc = x_ref[...]
    off = 0
    for p in parts:
        wp = p.shape[1]
        acc = acc + _dot(p[...], w_ref[off:off + wp, :])
        off += wp
    if final:
        gf_ref = refs[2 + n_parts]
        ms = jnp.mean(acc * acc, axis=-1, keepdims=True)
        acc = acc * lax.rsqrt(ms + EPS) * gf_ref[...]
    o_ref[...] = acc


def _outproj(x2, parts, w, g_final=None):
    n, d = x2.shape
    final = g_final is not None
    in_specs = [pl.BlockSpec((PROJ_TM, d), lambda i: (i, 0))]
    in_specs += [pl.BlockSpec((PROJ_TM, p.shape[1]), lambda i: (i, 0)) for p in parts]
    in_specs += [pl.BlockSpec(w.shape, lambda i: (0, 0))]
    args = [x2, *parts, w]
    if final:
        in_specs += [pl.BlockSpec((1, d), lambda i: (0, 0))]
        args += [g_final.reshape(1, d)]
    return pl.pallas_call(
        functools.partial(_outproj_kernel, n_parts=len(parts), final=final),
        grid=(n // PROJ_TM,),
        in_specs=in_specs,
        out_specs=pl.BlockSpec((PROJ_TM, d), lambda i: (i, 0)),
        out_shape=jax.ShapeDtypeStruct((n, d), F32),
        compiler_params=_params(("parallel",)),
        name="outproj",
    )(*args)


def _memkv_kernel(m_ref, g_ref, w_ref, o_ref):
    x = m_ref[0]
    ms = jnp.mean(x * x, axis=-1, keepdims=True)
    h = (x * lax.rsqrt(ms + EPS) * g_ref[...]).astype(BF16)
    o_ref[0, 0] = _dot(h, w_ref[0]).astype(o_ref.dtype)


def _memkv(mem, g, w_kv):
    b, nm, d = mem.shape
    depth, _, wkv = w_kv.shape
    return pl.pallas_call(
        _memkv_kernel,
        grid=(depth, b),
        in_specs=[
            pl.BlockSpec((1, nm, d), lambda l, i: (i, 0, 0)),
            pl.BlockSpec((1, d), lambda l, i: (0, 0)),
            pl.BlockSpec((1, d, wkv), lambda l, i: (l, 0, 0)),
        ],
        out_specs=pl.BlockSpec((1, 1, nm, wkv), lambda l, i: (l, i, 0, 0)),
        out_shape=jax.ShapeDtypeStruct((depth, b, nm, wkv), BF16),
        compiler_params=_params(("parallel", "parallel")),
        name="memkv",
    )(mem, g.reshape(1, d), w_kv)


def _memattn_kernel(q_ref, g_ref, kv_ref, o_ref):
    q = q_ref[0].astype(F32) * (HEAD_DIM_M ** -0.5)
    outs = []
    for h in range(N_HEADS_M):
        sl = slice(h * HEAD_DIM_M, (h + 1) * HEAD_DIM_M)
        qh = q[:, sl].astype(BF16)
        kh = kv_ref[0, 0, :, h * HEAD_DIM_M:(h + 1) * HEAD_DIM_M]
        vh = kv_ref[0, 0, :, W_M + h * HEAD_DIM_M:W_M + (h + 1) * HEAD_DIM_M]
        s = _dot_nt(qh, kh)
        m = jnp.max(s, axis=-1, keepdims=True)
        p = jnp.exp(s - m)
        l = jnp.sum(p, axis=-1, keepdims=True)
        outs.append(_dot(p.astype(BF16), vh) / l)
    o = jnp.concatenate(outs, axis=-1)
    o_ref[0] = (o * _silu(g_ref[0].astype(F32))).astype(o_ref.dtype)


def _memattn(proj, kv, layer, q_off, g_off):
    b, t, _ = proj.shape
    qb, gb = q_off // W_M, g_off // W_M
    return pl.pallas_call(
        _memattn_kernel,
        grid=(b, t // MEM_TQ),
        in_specs=[
            pl.BlockSpec((1, MEM_TQ, W_M), lambda i, j: (i, j, qb)),
            pl.BlockSpec((1, MEM_TQ, W_M), lambda i, j: (i, j, gb)),
            pl.BlockSpec((1, 1, N_MEM, 2 * W_M), lambda i, j: (layer, i, 0, 0)),
        ],
        out_specs=pl.BlockSpec((1, MEM_TQ, W_M), lambda i, j: (i, j, 0)),
        out_shape=jax.ShapeDtypeStruct((b, t, W_M), BF16),
        compiler_params=_params(("parallel", "parallel")),
        name="memattn",
    )(proj, proj, kv)


def _win_bias():
    i = np.arange(BLOCK)[:, None]
    j = np.arange(BLOCK)[None, :]
    out = np.empty((3, N_Q_A, 3, BLOCK, BLOCK), np.float32)
    for e in range(3):
        for h in range(N_Q_A):
            slope = 2.0 ** (-8.0 * (h + 1) / N_Q_A)
            for blk in range(3):
                dist = np.abs(i - j - (blk - 1) * BLOCK)
                valid = dist <= WINDOW
                if (e == 0 and blk == 0) or (e == 2 and blk == 2):
                    valid = np.zeros_like(valid)
                out[e, h, blk] = np.where(valid, -slope * dist * LOG2E, MASK_VALUE)
    return jnp.asarray(out)


def _winattn_kernel(sink_ref, q_ref, g_ref, k_ref, v_ref, bias_ref, o_ref, *, seq):
    c = pl.program_id(1)
    nb = seq // BLOCK
    hd = HEAD_DIM_A
    lo = lax.broadcasted_iota(jnp.int32, (BLOCK, LANE), 1) < hd
    q = (q_ref[0].astype(F32) * (hd ** -0.5 * LOG2E)).astype(BF16)
    kvar, vvar = [], []
    for off in (-1, 0, 1):
        start = pl.multiple_of(jnp.clip(c + off, 0, nb - 1) * BLOCK, BLOCK)
        kb = k_ref[0, pl.ds(start, BLOCK), :].astype(F32)
        vb = v_ref[0, pl.ds(start, BLOCK), :].astype(F32)
        kr = pltpu.roll(kb, hd, axis=1)
        vr = pltpu.roll(vb, hd, axis=1)
        kvar.append({(0, 0): jnp.where(lo, kb, 0.0).astype(BF16), (0, 1): jnp.where(lo, 0.0, kr).astype(BF16),
                     (1, 0): jnp.where(lo, kr, 0.0).astype(BF16), (1, 1): jnp.where(lo, 0.0, kb).astype(BF16)})
        vvar.append({(0, 0): jnp.where(lo, vb, 1.0).astype(BF16), (0, 1): jnp.where(lo, 1.0, vr).astype(BF16),
                     (1, 0): jnp.where(lo, vr, 1.0).astype(BF16), (1, 1): jnp.where(lo, 1.0, vb).astype(BF16)})
    groups = [(n, par) for n in range(N_KV_A) for par in range(2)]
    s = {}
    for n, par in groups:
        lhs = jnp.concatenate([q[:, (2 * n) * LANE:(2 * n + 1) * LANE],
                               q[:, (2 * n + 1) * LANE:(2 * n + 2) * LANE]], axis=0)
        for blk in range(3):
            s[n, par, blk] = _dot_nt(lhs, kvar[blk][n, par])
    p, sink_term = {}, {}
    for n, par in groups:
        for r in range(2):
            h = 4 * n + 2 * r + par
            sink2 = sink_ref[h] * LOG2E
            sb = [s[n, par, blk][r * BLOCK:(r + 1) * BLOCK] + bias_ref[0, h, blk] for blk in range(3)]
            m = jnp.max(jnp.maximum(jnp.maximum(sb[0], sb[1]), sb[2]), axis=-1, keepdims=True)
            m = jnp.maximum(m, sink2)
            p[h] = [jnp.exp2(x - m).astype(BF16) for x in sb]
            sink_term[h] = jnp.exp2(sink2 - m)
    res = {}
    for n, par in groups:
        acc = None
        for blk in range(3):
            lhs = jnp.concatenate([p[4 * n + par][blk], p[4 * n + 2 + par][blk]], axis=0)
            d = _dot(lhs, vvar[blk][n, par])
            acc = d if acc is None else acc + d
        res[n, par] = acc
    for j in range(N_Q_A // 2):
        n, r = j // 2, j % 2
        r_even = res[n, 0][r * BLOCK:(r + 1) * BLOCK]
        r_odd = res[n, 1][r * BLOCK:(r + 1) * BLOCK]
        vals = jnp.where(lo, r_even, r_odd)
        sums = pltpu.roll(jnp.where(lo, r_odd, r_even), hd, axis=1)
        denom = sums + jnp.where(lo, sink_term[2 * j], sink_term[2 * j + 1])
        gate = g_ref[0, :, j * LANE:(j + 1) * LANE].astype(F32)
        o_ref[0, :, j * LANE:(j + 1) * LANE] = (vals / denom * _silu(gate)).astype(o_ref.dtype)


def _winattn(proj, sink, bias):
    b, t, _ = proj.shape
    nb = t // BLOCK
    assert nb >= 2
    edge = lambda i, j: (jnp.where(j == 0, 0, jnp.where(j == nb - 1, 2, 1)), 0, 0, 0, 0)
    return pl.pallas_call(
        functools.partial(_winattn_kernel, seq=t),
        grid=(b, nb),
        in_specs=[
            pl.BlockSpec(memory_space=pltpu.SMEM),
            pl.BlockSpec((1, BLOCK, W_A), lambda i, j: (i, j, E_QA // W_A)),
            pl.BlockSpec((1, BLOCK, W_A), lambda i, j: (i, j, E_GA // W_A)),
            pl.BlockSpec((1, t, W_KV_A), lambda i, j: (i, 0, E_KA // W_KV_A)),
            pl.BlockSpec((1, t, W_KV_A), lambda i, j: (i, 0, E_VA // W_KV_A)),
            pl.BlockSpec((1, N_Q_A, 3, BLOCK, BLOCK), edge),
        ],
        out_specs=pl.BlockSpec((1, BLOCK, W_A), lambda i, j: (i, j, 0)),
        out_shape=jax.ShapeDtypeStruct((b, t, W_A), BF16),
        compiler_params=_params(("parallel", "arbitrary")),
        name="winattn",
    )(sink, proj, proj, proj, proj, bias)


def _scan_constants(rev):
    c = CHUNK
    t = np.arange(c)[:, None]
    u = np.arange(c)[None, :]
    mats = []
    mats.append((u >= t) if rev else (u <= t))
    masks = [t == u]
    for lvl in range(1, N_LEVELS + 1):
        blk, half = 2 ** lvl, 2 ** (lvl - 1)
        m = (t // blk) * blk + half
        if rev:
            mat = np.where(t < m, (u >= t) & (u < m), (u >= m) & (u < t))
            mask = (t // blk == u // blk) & (t % blk < half) & (u % blk >= half)
        else:
            mat = np.where(t >= m, (u >= m) & (u <= t), (u > t) & (u < m))
            mask = (t // blk == u // blk) & (t % blk >= half) & (u % blk < half)
        mats.append(mat)
        masks.append(mask)
    mats.append((u < t) if rev else (u > t))
    mall = np.concatenate(mats, axis=0).astype(np.float32)
    mall3 = np.concatenate([mall, mall, mall], axis=1)
    return jnp.asarray(mall3, dtype=BF16), jnp.asarray(np.stack(masks).astype(np.float32))


def _split3(g):
    g1 = g.astype(BF16)
    rem = g - g1.astype(F32)
    g2 = rem.astype(BF16)
    g3 = (rem - g2.astype(F32)).astype(BF16)
    return jnp.concatenate([g1, g2, g3], axis=0)


def _scan_group(dirs):
    c = CHUNK
    es = []
    for chunks, mall3_ref, _, _, _ in dirs:
        g123 = jnp.concatenate([_split3(g * LOG2E) for (_, _, _, g) in chunks], axis=1)
        es.append(jnp.exp2(_dot(mall3_ref[...], g123)))

    def e_rows(di, u, blk):
        dk = dirs[di][0][u][0].shape[1]
        return es[di][blk * c:(blk + 1) * c, u * dk:(u + 1) * dk]

    o_intra, w_inc = {}, {}
    for u in range(SCAN_UNROLL):
        for di, (chunks, _, mask_ref, _, _) in enumerate(dirs):
            q, k, v, _ = chunks[u]
            a = mask_ref[0] * _dot_nt(q.astype(BF16), k.astype(BF16))
            for lvl in range(1, N_LEVELS + 1):
                e_l = e_rows(di, u, lvl)
                a = a + mask_ref[lvl] * _dot_nt((q * e_l).astype(BF16), (k * e_l).astype(BF16))
            vb = v.astype(BF16)
            o_intra[di, u] = _dot(a.astype(BF16), vb)
            w_inc[di, u] = _dot_tn(vb, (k * e_rows(di, u, N_LEVELS + 1)).astype(BF16))
    sts = [d[3][...] for d in dirs]
    outs = [[None] * SCAN_UNROLL for _ in dirs]
    for u in range(SCAN_UNROLL):
        for di, (chunks, _, _, _, rev) in enumerate(dirs):
            q = chunks[u][0]
            e_b = e_rows(di, u, 0)
            e_tot = e_b[0:1] if rev else e_b[c - 1:c]
            st = sts[di]
            outs[di][u] = o_intra[di, u] + _dot_nt((q * e_b).astype(BF16), st.astype(BF16))
            sts[di] = st * e_tot + w_inc[di, u]
    for di, d in enumerate(dirs):
        d[3][...] = sts[di]
    return outs


def _group_norm_gate(o, norm_g, gate):
    ms = jnp.mean(o * o, axis=-1, keepdims=True)
    return o * lax.rsqrt(ms + EPS) * norm_g * _silu(gate)


def _scan_loop(prep_fwd, prep_bwd, finish, refs, seq):
    mallf_ref, maskf_ref, mallb_ref, maskb_ref, oaccf_ref, oaccb_ref, stf_ref, stb_ref = refs
    nch = seq // CHUNK
    stf_ref[...] = jnp.zeros_like(stf_ref)
    stb_ref[...] = jnp.zeros_like(stb_ref)

    def body(n, carry):
        rows_f = [pl.ds(pl.multiple_of((n * SCAN_UNROLL + u) * CHUNK, CHUNK), CHUNK)
                  for u in range(SCAN_UNROLL)]
        rows_b = [pl.ds(pl.multiple_of((nch - 1 - n * SCAN_UNROLL - u) * CHUNK, CHUNK), CHUNK)
                  for u in range(SCAN_UNROLL)]
        outs = _scan_group([
            ([prep_fwd(rs) for rs in rows_f], mallf_ref, maskf_ref, stf_ref, False),
            ([prep_bwd(rs) for rs in rows_b], mallb_ref, maskb_ref, stb_ref, True)])
        for u in range(SCAN_UNROLL):
            oaccf_ref[rows_f[u], :] = outs[0][u]
            oaccb_ref[rows_b[u], :] = outs[1][u]
        return carry

    lax.fori_loop(0, nch // SCAN_UNROLL, body, 0)

    def fin_body(n, carry):
        rs = pl.ds(pl.multiple_of(n * FIN_ROWS, FIN_ROWS), FIN_ROWS)
        finish(rs, oaccf_ref[rs, :] + oaccb_ref[rs, :])
        return carry

    lax.fori_loop(0, seq // FIN_ROWS, fin_body, 0)


def _hgrn_kernel(q_ref, zf_ref, zb_ref, i_ref, gate_ref, lb_ref, ng_ref,
                 mallf_ref, maskf_ref, mallb_ref, maskb_ref, o_ref, *scratch,
                 layer_idx, n_even, seq):
    lbp = lb_ref[...].astype(F32)
    rows = [[lbp[l * 2 + d:l * 2 + d + 1] for l in range(n_even)] for d in range(2)]
    lower = []
    for d in range(2):
        mx = functools.reduce(jnp.maximum, rows[d])
        ex = [jnp.exp(r - mx) for r in rows[d]]
        tot = functools.reduce(lambda a, b: a + b, ex)
        lbs = [e / tot for e in ex]
        cum = functools.reduce(lambda a, b: a + b, lbs[:layer_idx + 1])
        lower.append(cum - lbs[0])

    def prep(rs, z_ref, lb):
        sg = jax.nn.sigmoid(z_ref[0, rs, :].astype(F32))
        f = lb + (1.0 - lb) * sg
        g = jnp.log(jnp.maximum(f, MIN_GATE))
        k = (1.0 - lb) * (1.0 - sg)
        q = _silu(q_ref[0, rs, :].astype(F32))
        return q, k, i_ref[0, rs, :].astype(F32), g

    def finish(rs, o):
        y = _group_norm_gate(o, ng_ref[...], gate_ref[0, rs, :].astype(F32))
        o_ref[0, rs, :] = y.astype(o_ref.dtype)

    _scan_loop(lambda rs: prep(rs, zf_ref, lower[0]), lambda rs: prep(rs, zb_ref, lower[1]),
               finish, (mallf_ref, maskf_ref, mallb_ref, maskb_ref, *scratch), seq)


def _const_spec(a):
    nd = a.ndim
    return pl.BlockSpec(a.shape, lambda i, j: (0,) * nd)


def _hgrn(proj, lb_param, norm_g, layer_idx, consts):
    b, t, _ = proj.shape
    n_even = lb_param.shape[0]
    hd = HEAD_DIM_B
    col = lambda off: pl.BlockSpec((1, t, hd), lambda i, j: (i, 0, off // hd + j))
    return pl.pallas_call(
        functools.partial(_hgrn_kernel, layer_idx=layer_idx, n_even=n_even, seq=t),
        grid=(b, N_HEADS_B),
        in_specs=[col(E_QB), col(E_ZF), col(E_ZB), col(E_IB), col(E_GB),
                  pl.BlockSpec((n_even * 2, hd), lambda i, j: (0, j)),
                  pl.BlockSpec((1, hd), lambda i, j: (0, j)),
                  *[_const_spec(a) for a in consts]],
        out_specs=pl.BlockSpec((1, t, hd), lambda i, j: (i, 0, j)),
        out_shape=jax.ShapeDtypeStruct((b, t, W_B), BF16),
        scratch_shapes=[pltpu.VMEM((t, hd), F32), pltpu.VMEM((t, hd), F32),
                        pltpu.VMEM((hd, hd), F32), pltpu.VMEM((hd, hd), F32)],
        compiler_params=_params(("parallel", "parallel")),
        name="hgrn",
    )(proj, proj, proj, proj, proj, lb_param.reshape(n_even * 2, W_B),
      norm_g.reshape(1, W_B), *consts)


def _gla_kernel(q_ref, k_ref, v_ref, gate_ref, r_ref, wuf_ref, wub_ref, bf_ref, bb_ref, ng_ref,
                mallf_ref, maskf_ref, mallb_ref, maskb_ref, o_ref, *scratch, seq):
    def prep(rs, wu_ref, bias_ref):
        r = r_ref[0, rs, :].astype(F32)
        x = jnp.dot(r, wu_ref[...], preferred_element_type=F32,
                    precision=lax.Precision.HIGHEST) + bias_ref[...]
        g = (jnp.minimum(x, 0.0) - jnp.log(1.0 + jnp.exp(-jnp.abs(x)))) * (1.0 / GATE_TEMP)
        q = q_ref[0, rs, :].astype(F32) * (DK_C ** -0.5)
        return q, k_ref[0, rs, :].astype(F32), v_ref[0, rs, :].astype(F32), g

    def finish(rs, o):
        y = _group_norm_gate(o, ng_ref[...], gate_ref[0, rs, :].astype(F32))
        o_ref[0, rs, :] = y.astype(o_ref.dtype)

    _scan_loop(lambda rs: prep(rs, wuf_ref, bf_ref), lambda rs: prep(rs, wub_ref, bb_ref),
               finish, (mallf_ref, maskf_ref, mallb_ref, maskb_ref, *scratch), seq)


def _gla(proj, w_up, b_gate, norm_g, consts):
    b, t, _ = proj.shape
    dk, dv = DK_C, DV_C
    wuf = jnp.zeros((LANE, WK_C), F32).at[0:GATE_RANK].set(w_up[0])
    wub = jnp.zeros((LANE, WK_C), F32).at[GATE_RANK:2 * GATE_RANK].set(w_up[1])
    colk = lambda off: pl.BlockSpec((1, t, dk), lambda i, j: (i, 0, off // dk + j))
    colv = lambda off: pl.BlockSpec((1, t, dv), lambda i, j: (i, 0, off // dv + j))
    return pl.pallas_call(
        functools.partial(_gla_kernel, seq=t),
        grid=(b, N_HEADS_C),
        in_specs=[colk(O_QC), colk(O_KC), colv(O_VC), colv(O_GC),
                  pl.BlockSpec((1, t, LANE), lambda i, j: (i, 0, O_R // LANE)),
                  pl.BlockSpec((LANE, dk), lambda i, j: (0, j)),
                  pl.BlockSpec((LANE, dk), lambda i, j: (0, j)),
                  pl.BlockSpec((1, dk), lambda i, j: (0, j)),
                  pl.BlockSpec((1, dk), lambda i, j: (0, j)),
                  pl.BlockSpec((1, dv), lambda i, j: (0, j)),
                  *[_const_spec(a) for a in consts]],
        out_specs=pl.BlockSpec((1, t, dv), lambda i, j: (i, 0, j)),
        out_shape=jax.ShapeDtypeStruct((b, t, WV_C), BF16),
        scratch_shapes=[pltpu.VMEM((t, dv), F32), pltpu.VMEM((t, dv), F32),
                        pltpu.VMEM((dv, dk), F32), pltpu.VMEM((dv, dk), F32)],
        compiler_params=_params(("parallel", "parallel")),
        name="gla",
    )(proj, proj, proj, proj, proj, wuf, wub, b_gate[0].reshape(1, WK_C),
      b_gate[1].reshape(1, WK_C), norm_g.reshape(1, WV_C), *consts)


def _even_w_in(w):
    w = w.astype(BF16)
    return jnp.concatenate([w[:, 0:512], w[:, 768:4864], w[:, 512:768]], axis=1)


def _odd_w_in(w):
    w = w.astype(BF16)
    pad = jnp.zeros((w.shape[0], ODD_IN_PAD - 4128), BF16)
    return jnp.concatenate([w[:, 0:3072], w[:, 3104:4128], w[:, 3072:3104], pad], axis=1)


def kernel(x, mem, norm_even, w_in_even, sink, lb_param, hgrn_norm, w_out_even, norm_odd,
           w_in_odd, w_gate_up, b_gate, gla_norm, w_out_odd, mem_norm, w_mem_kv, final_norm):
    b, t, d = x.shape
    consts = (*_scan_constants(False), *_scan_constants(True))
    win_bias = _win_bias()
    kv_all = _memkv(mem, mem_norm, w_mem_kv.astype(BF16))
    x2 = x.reshape(b * t, d)
    for l in range(DEPTH):
        i = l // 2
        g_final = final_norm if l == DEPTH - 1 else None
        if l % 2 == 0:
            proj = _inproj(x2, norm_even[i], _even_w_in(w_in_even[i])).reshape(b, t, EVEN_IN)
            a = _winattn(proj, sink[i], win_bias)
            o = _hgrn(proj, lb_param, hgrn_norm[i], i, consts)
            mo = _memattn(proj, kv_all, l, E_QM, E_GM)
            parts = [a.reshape(b * t, W_A), o.reshape(b * t, W_B), mo.reshape(b * t, W_M)]
            x2 = _outproj(x2, parts, w_out_even[i].astype(BF16), g_final)
        else:
            proj = _inproj(x2, norm_odd[i], _odd_w_in(w_in_odd[i])).reshape(b, t, ODD_IN_PAD)
            o = _gla(proj, w_gate_up[i], b_gate[i], gla_norm[i], consts)
            mo = _memattn(proj, kv_all, l, O_QM, O_GM)
            parts = [o.reshape(b * t, WV_C), mo.reshape(b * t, W_M)]
            x2 = _outproj(x2, parts, w_out_odd[i].astype(BF16), g_final)
    return x2.reshape(b, t, d)
```

```python
import functools

import numpy as np
import jax
import jax.numpy as jnp
from jax import lax
from jax.experimental import pallas as pl
from jax.experimental.pallas import tpu as pltpu

F32 = jnp.float32
BF16 = jnp.bfloat16

D_MODEL = 1024
DEPTH = 4
N_Q_A, N_KV_A, HEAD_DIM_A = 8, 2, 64
W_A, W_KV_A = 512, 128
WINDOW = 128
BLOCK = 128
N_HEADS_B, HEAD_DIM_B, W_B = 4, 128, 512
N_HEADS_C, DK_C, DV_C = 4, 128, 256
WK_C, WV_C = 512, 1024
GATE_RANK = 16
GATE_TEMP = 16.0
N_MEM, N_HEADS_M, HEAD_DIM_M, W_M = 256, 4, 128, 512
EPS = 1e-6
MASK_VALUE = -1e30
MIN_GATE = 1e-30
LOG2E = 1.4426950408889634

EVEN_IN = 4864
E_QA, E_GA, E_QB, E_ZF, E_ZB, E_IB, E_GB, E_QM, E_GM, E_KA, E_VA = (
    0, 512, 1024, 1536, 2048, 2560, 3072, 3584, 4096, 4608, 4736)
ODD_IN_PAD = 4352
O_QC, O_KC, O_VC, O_GC, O_QM, O_GM, O_R = 0, 512, 1024, 2048, 3072, 3584, 4096

LANE = 128
PROJ_DTYPE = F32
PROJ_TM = 256
PROJ_TN = 256
CHUNK = 64
N_LEVELS = 6
LEVEL_PAIRS = ((0, 1), (2, 3), (4, 5), (6, None))
MXU_LEVELS = 3
SCAN_UNROLL = 8
FIN_ROWS = 256
MEM_TQ = 512
VMEM_LIMIT = 56 * 1024 * 1024

_NT = (((1,), (1,)), ((), ()))
_TN = (((0,), (0,)), ((), ()))


def _dot(a, b):
    return jnp.dot(a, b, preferred_element_type=F32)


def _dot_nt(a, b):
    return lax.dot_general(a, b, _NT, preferred_element_type=F32)


def _dot_tn(a, b):
    return lax.dot_general(a, b, _TN, preferred_element_type=F32)


def _silu(x):
    return x * jax.nn.sigmoid(x)


def _params(sem):
    return pltpu.CompilerParams(dimension_semantics=sem, vmem_limit_bytes=VMEM_LIMIT)


def _inproj_kernel(x_ref, g_ref, w_ref, o_ref):
    x = x_ref[...]
    ms = jnp.mean(x * x, axis=-1, keepdims=True)
    h = (x * lax.rsqrt(ms + EPS) * g_ref[...]).astype(BF16)
    n_out = o_ref.shape[1]
    for c in range(0, n_out, PROJ_TN):
        o_ref[:, c:c + PROJ_TN] = _dot(h, w_ref[:, c:c + PROJ_TN]).astype(o_ref.dtype)


def _inproj(x2, g, w):
    n, d = x2.shape
    n_out = w.shape[1]
    return pl.pallas_call(
        _inproj_kernel,
        grid=(n // PROJ_TM,),
        in_specs=[
            pl.BlockSpec((PROJ_TM, d), lambda i: (i, 0)),
            pl.BlockSpec((1, d), lambda i: (0, 0)),
            pl.BlockSpec((d, n_out), lambda i: (0, 0)),
        ],
        out_specs=pl.BlockSpec((PROJ_TM, n_out), lambda i: (i, 0)),
        out_shape=jax.ShapeDtypeStruct((n, n_out), PROJ_DTYPE),
        compiler_params=_params(("parallel",)),
        name="inproj",
    )(x2, g.reshape(1, d), w)


def _outproj_kernel(*refs, n_parts, final):
    x_ref = refs[0]
    parts = refs[1:1 + n_parts]
    w_ref = refs[1 + n_parts]
    o_ref = refs[-1]
    acc = x_ref[...]
    off = 0
    for p in parts:
        wp = p.shape[1]
        acc = acc + _dot(p[...], w_ref[off:off + wp, :])
        off += wp
    if final:
        gf_ref = refs[2 + n_parts]
        ms = jnp.mean(acc * acc, axis=-1, keepdims=True)
        acc = acc * lax.rsqrt(ms + EPS) * gf_ref[...]
    o_ref[...] = acc


def _outproj(x2, parts, w, g_final=None):
    n, d = x2.shape
    final = g_final is not None
    in_specs = [pl.BlockSpec((PROJ_TM, d), lambda i: (i, 0))]
    in_specs += [pl.BlockSpec((PROJ_TM, p.shape[1]), lambda i: (i, 0)) for p in parts]
    in_specs += [pl.BlockSpec(w.shape, lambda i: (0, 0))]
    args = [x2, *parts, w]
    if final:
        in_specs += [pl.BlockSpec((1, d), lambda i: (0, 0))]
        args += [g_final.reshape(1, d)]
    return pl.pallas_call(
        functools.partial(_outproj_kernel, n_parts=len(parts), final=final),
        grid=(n // PROJ_TM,),
        in_specs=in_specs,
        out_specs=pl.BlockSpec((PROJ_TM, d), lambda i: (i, 0)),
        out_shape=jax.ShapeDtypeStruct((n, d), F32),
        compiler_params=_params(("parallel",)),
        name="outproj",
    )(*args)


def _memkv_kernel(m_ref, g_ref, w_ref, o_ref):
    x = m_ref[0]
    ms = jnp.mean(x * x, axis=-1, keepdims=True)
    h = (x * lax.rsqrt(ms + EPS) * g_ref[...]).astype(BF16)
    o_ref[0, 0] = _dot(h, w_ref[0]).astype(o_ref.dtype)


def _memkv(mem, g, w_kv):
    b, nm, d = mem.shape
    depth, _, wkv = w_kv.shape
    return pl.pallas_call(
        _memkv_kernel,
        grid=(depth, b),
        in_specs=[
            pl.BlockSpec((1, nm, d), lambda l, i: (i, 0, 0)),
            pl.BlockSpec((1, d), lambda l, i: (0, 0)),
            pl.BlockSpec((1, d, wkv), lambda l, i: (l, 0, 0)),
        ],
        out_specs=pl.BlockSpec((1, 1, nm, wkv), lambda l, i: (l, i, 0, 0)),
        out_shape=jax.ShapeDtypeStruct((depth, b, nm, wkv), BF16),
        compiler_params=_params(("parallel", "parallel")),
        name="memkv",
    )(mem, g.reshape(1, d), w_kv)


def _memattn_kernel(q_ref, g_ref, kv_ref, o_ref):
    q = q_ref[0].astype(F32) * (HEAD_DIM_M ** -0.5)
    outs = []
    for h in range(N_HEADS_M):
        sl = slice(h * HEAD_DIM_M, (h + 1) * HEAD_DIM_M)
        qh = q[:, sl].astype(BF16)
        kh = kv_ref[0, 0, :, h * HEAD_DIM_M:(h + 1) * HEAD_DIM_M]
        vh = kv_ref[0, 0, :, W_M + h * HEAD_DIM_M:W_M + (h + 1) * HEAD_DIM_M]
        s = _dot_nt(qh, kh)
        m = jnp.max(s, axis=-1, keepdims=True)
        p = jnp.exp(s - m)
        l = jnp.sum(p, axis=-1, keepdims=True)
        outs.append(_dot(p.astype(BF16), vh) / l)
    o = jnp.concatenate(outs, axis=-1)
    o_ref[0] = (o * _silu(g_ref[0].astype(F32))).astype(o_ref.dtype)


def _memattn(proj, kv, layer, q_off, g_off):
    b, t, _ = proj.shape
    qb, gb = q_off // W_M, g_off // W_M
    return pl.pallas_call(
        _memattn_kernel,
        grid=(b, t // MEM_TQ),
        in_specs=[
            pl.BlockSpec((1, MEM_TQ, W_M), lambda i, j: (i, j, qb)),
            pl.BlockSpec((1, MEM_TQ, W_M), lambda i, j: (i, j, gb)),
            pl.BlockSpec((1, 1, N_MEM, 2 * W_M), lambda i, j: (layer, i, 0, 0)),
        ],
        out_specs=pl.BlockSpec((1, MEM_TQ, W_M), lambda i, j: (i, j, 0)),
        out_shape=jax.ShapeDtypeStruct((b, t, W_M), BF16),
        compiler_params=_params(("parallel", "parallel")),
        name="memattn",
    )(proj, proj, kv)


def _win_bias():
    i = np.arange(BLOCK)[:, None]
    j = np.arange(BLOCK)[None, :]
    out = np.empty((3, N_Q_A, 3, BLOCK, BLOCK), np.float32)
    for e in range(3):
        for h in range(N_Q_A):
            slope = 2.0 ** (-8.0 * (h + 1) / N_Q_A)
            for blk in range(3):
                dist = np.abs(i - j - (blk - 1) * BLOCK)
                valid = dist <= WINDOW
                if (e == 0 and blk == 0) or (e == 2 and blk == 2):
                    valid = np.zeros_like(valid)
                out[e, h, blk] = np.where(valid, -slope * dist * LOG2E, MASK_VALUE)
    return jnp.asarray(out)


def _winattn_kernel(sink_ref, q_ref, g_ref, k_ref, v_ref, bias_ref, o_ref, *, seq):
    c = pl.program_id(1)
    nb = seq // BLOCK
    hd = HEAD_DIM_A
    lo = lax.broadcasted_iota(jnp.int32, (BLOCK, LANE), 1) < hd
    q = (q_ref[0].astype(F32) * (hd ** -0.5 * LOG2E)).astype(BF16)
    kvar, vvar = [], []
    for off in (-1, 0, 1):
        start = pl.multiple_of(jnp.clip(c + off, 0, nb - 1) * BLOCK, BLOCK)
        kb = k_ref[0, pl.ds(start, BLOCK), :].astype(F32)
        vb = v_ref[0, pl.ds(start, BLOCK), :].astype(F32)
        kr = pltpu.roll(kb, hd, axis=1)
        vr = pltpu.roll(vb, hd, axis=1)
        kvar.append({(0, 0): jnp.where(lo, kb, 0.0).astype(BF16), (0, 1): jnp.where(lo, 0.0, kr).astype(BF16),
                     (1, 0): jnp.where(lo, kr, 0.0).astype(BF16), (1, 1): jnp.where(lo, 0.0, kb).astype(BF16)})
        vvar.append({(0, 0): jnp.where(lo, vb, 1.0).astype(BF16), (0, 1): jnp.where(lo, 1.0, vr).astype(BF16),
                     (1, 0): jnp.where(lo, vr, 1.0).astype(BF16), (1, 1): jnp.where(lo, 1.0, vb).astype(BF16)})
    groups = [(n, par) for n in range(N_KV_A) for par in range(2)]
    s = {}
    for n, par in groups:
        lhs = jnp.concatenate([q[:, (2 * n) * LANE:(2 * n + 1) * LANE],
                               q[:, (2 * n + 1) * LANE:(2 * n + 2) * LANE]], axis=0)
        for blk in range(3):
            s[n, par, blk] = _dot_nt(lhs, kvar[blk][n, par])
    p, sink_term = {}, {}
    for n, par in groups:
        for r in range(2):
            h = 4 * n + 2 * r + par
            sink2 = sink_ref[h] * LOG2E
            sb = [s[n, par, blk][r * BLOCK:(r + 1) * BLOCK] + bias_ref[0, h, blk] for blk in range(3)]
            m = jnp.max(jnp.maximum(jnp.maximum(sb[0], sb[1]), sb[2]), axis=-1, keepdims=True)
            m = jnp.maximum(m, sink2)
            p[h] = [jnp.exp2(x - m).astype(BF16) for x in sb]
            sink_term[h] = jnp.exp2(sink2 - m)
    res = {}
    for n, par in groups:
        acc = None
        for blk in range(3):
            lhs = jnp.concatenate([p[4 * n + par][blk], p[4 * n + 2 + par][blk]], axis=0)
            d = _dot(lhs, vvar[blk][n, par])
            acc = d if acc is None else acc + d
        res[n, par] = acc
    for j in range(N_Q_A // 2):
        n, r = j // 2, j % 2
        r_even = res[n, 0][r * BLOCK:(r + 1) * BLOCK]
        r_odd = res[n, 1][r * BLOCK:(r + 1) * BLOCK]
        vals = jnp.where(lo, r_even, r_odd)
        sums = pltpu.roll(jnp.where(lo, r_odd, r_even), hd, axis=1)
        denom = sums + jnp.where(lo, sink_term[2 * j], sink_term[2 * j + 1])
        gate = g_ref[0, :, j * LANE:(j + 1) * LANE].astype(F32)
        o_ref[0, :, j * LANE:(j + 1) * LANE] = (vals / denom * _silu(gate)).astype(o_ref.dtype)


def _winattn(proj, sink, bias):
    b, t, _ = proj.shape
    nb = t // BLOCK
    assert nb >= 2
    edge = lambda i, j: (jnp.where(j == 0, 0, jnp.where(j == nb - 1, 2, 1)), 0, 0, 0, 0)
    return pl.pallas_call(
        functools.partial(_winattn_kernel, seq=t),
        grid=(b, nb),
        in_specs=[
            pl.BlockSpec(memory_space=pltpu.SMEM),
            pl.BlockSpec((1, BLOCK, W_A), lambda i, j: (i, j, E_QA // W_A)),
            pl.BlockSpec((1, BLOCK, W_A), lambda i, j: (i, j, E_GA // W_A)),
            pl.BlockSpec((1, t, W_KV_A), lambda i, j: (i, 0, E_KA // W_KV_A)),
            pl.BlockSpec((1, t, W_KV_A), lambda i, j: (i, 0, E_VA // W_KV_A)),
            pl.BlockSpec((1, N_Q_A, 3, BLOCK, BLOCK), edge),
        ],
        out_specs=pl.BlockSpec((1, BLOCK, W_A), lambda i, j: (i, j, 0)),
        out_shape=jax.ShapeDtypeStruct((b, t, W_A), BF16),
        compiler_params=_params(("parallel", "arbitrary")),
        name="winattn",
    )(sink, proj, proj, proj, proj, bias)


def _scan_constants(rev):
    c = CHUNK
    t = np.arange(c)[:, None]
    u = np.arange(c)[None, :]
    mats = []
    mats.append((u >= t) if rev else (u <= t))
    masks = [t == u]
    for lvl in range(1, N_LEVELS + 1):
        blk, half = 2 ** lvl, 2 ** (lvl - 1)
        m = (t // blk) * blk + half
        if rev:
            mat = np.where(t < m, (u >= t) & (u < m), (u >= m) & (u < t))
            mask = (t // blk == u // blk) & (t % blk < half) & (u % blk >= half)
        else:
            mat = np.where(t >= m, (u >= m) & (u <= t), (u > t) & (u < m))
            mask = (t // blk == u // blk) & (t % blk >= half) & (u % blk < half)
        if lvl <= MXU_LEVELS:
            mats.append(mat)
        masks.append(mask)
    mall = np.concatenate(mats, axis=0).astype(np.float32)
    mall2 = np.concatenate([mall, mall], axis=1)
    none = np.zeros((c, c), bool)
    masks2 = np.stack([np.concatenate([masks[la], none if lb is None else masks[lb]], axis=1)
                       for la, lb in LEVEL_PAIRS]).astype(np.float32)
    return jnp.asarray(mall2, dtype=BF16), jnp.asarray(masks2)


def _level_exponent(b, lvl, rev):
    blk, half = 2 ** lvl, 2 ** (lvl - 1)
    parts = []
    for s in range(0, b.shape[0], blk):
        m = s + half
        lower, upper = b[s:m], b[m:s + blk]
        if rev:
            r = b[m:m + 1]
            parts += [lower - r, r - upper]
        else:
            r = b[m - 1:m]
            parts += [r - lower, upper - r]
    return jnp.concatenate(parts, axis=0)


def _split2(g):
    g1 = g.astype(BF16)
    g2 = (g - g1.astype(F32)).astype(BF16)
    return jnp.concatenate([g1, g2], axis=0)


def _scan_front(dirs):
    c = CHUNK
    e_tots, es = {}, {}
    for di, (chunks, mall2_ref, _, rev) in enumerate(dirs):
        g12 = jnp.concatenate([_split2(g * LOG2E) for (_, _, _, g) in chunks], axis=1)
        ex = _dot(mall2_ref[...], g12)
        for u, (q, _, _, _) in enumerate(chunks):
            dk = q.shape[1]
            exu = ex[:, u * dk:(u + 1) * dk]
            b = exu[0:c]
            b_tot = b[0:1] if rev else b[c - 1:c]
            full = jnp.concatenate(
                [exu] + [_level_exponent(b, lvl, rev) for lvl in range(MXU_LEVELS + 1, N_LEVELS + 1)]
                + [b_tot - b], axis=0)
            e_tots[di, u] = jnp.exp2(b_tot)
            es[di, u] = jnp.exp2(full).astype(BF16)

    def e_rows(di, u, blk):
        return es[di, u][blk * c:(blk + 1) * c]

    o_intra, w_inc = {}, {}
    for u in range(SCAN_UNROLL):
        for di, (chunks, _, mask_ref, _) in enumerate(dirs):
            q, k, v, _ = chunks[u]
            zeros = jnp.zeros_like(k)
            a2 = None
            for pi, (la, lb) in enumerate(LEVEL_PAIRS):
                qa, ka = (q, k) if la == 0 else (q * e_rows(di, u, la), k * e_rows(di, u, la))
                if lb is None:
                    lhs, rhs = qa, jnp.concatenate([ka, zeros], axis=0)
                else:
                    e_l = e_rows(di, u, lb)
                    lhs = jnp.concatenate([qa, q * e_l], axis=1)
                    rhs = jnp.concatenate([jnp.concatenate([ka, zeros], axis=1),
                                           jnp.concatenate([zeros, k * e_l], axis=1)], axis=0)
                p = mask_ref[pi] * _dot_nt(lhs, rhs)
                a2 = p if a2 is None else a2 + p
            o_intra[di, u] = _dot(a2.astype(BF16), jnp.concatenate([v, v], axis=0))
            w_inc[di, u] = _dot_tn(v, k * e_rows(di, u, N_LEVELS + 1))
    return [[(dirs[di][0][u][0] * e_rows(di, u, 0), o_intra[di, u], w_inc[di, u], e_tots[di, u])
             for u in range(SCAN_UNROLL)] for di in range(len(dirs))]


def _scan_back(pend, st_refs):
    sts = [r[...] for r in st_refs]
    outs = [[None] * SCAN_UNROLL for _ in st_refs]
    for u in range(SCAN_UNROLL):
        for di in range(len(st_refs)):
            qe, o_intra, w_inc, e_tot = pend[di][u]
            st = sts[di]
            outs[di][u] = o_intra + _dot_nt(qe, st.astype(BF16))
            sts[di] = st * e_tot + w_inc
    for r, st in zip(st_refs, sts):
        r[...] = st
    return outs


def _group_norm_gate(o, norm_g, gate):
    ms = jnp.mean(o * o, axis=-1, keepdims=True)
    return o * lax.rsqrt(ms + EPS) * norm_g * _silu(gate)


def _scan_loop(prep_fwd, prep_bwd, finish, refs, seq):
    mallf_ref, maskf_ref, mallb_ref, maskb_ref, oaccf_ref, oaccb_ref, stf_ref, stb_ref = refs
    nch = seq // CHUNK
    stf_ref[...] = jnp.zeros_like(stf_ref)
    stb_ref[...] = jnp.zeros_like(stb_ref)

    ngroups = nch // SCAN_UNROLL

    def rows(n):
        rows_f = [pl.ds(pl.multiple_of((n * SCAN_UNROLL + u) * CHUNK, CHUNK), CHUNK)
                  for u in range(SCAN_UNROLL)]
        rows_b = [pl.ds(pl.multiple_of((nch - 1 - n * SCAN_UNROLL - u) * CHUNK, CHUNK), CHUNK)
                  for u in range(SCAN_UNROLL)]
        return rows_f, rows_b

    def front(n):
        rows_f, rows_b = rows(n)
        return _scan_front([
            ([prep_fwd(rs) for rs in rows_f], mallf_ref, maskf_ref, False),
            ([prep_bwd(rs) for rs in rows_b], mallb_ref, maskb_ref, True)])

    def back(n, pend):
        rows_f, rows_b = rows(n)
        outs = _scan_back(pend, (stf_ref, stb_ref))
        for u in range(SCAN_UNROLL):
            oaccf_ref[rows_f[u], :] = outs[0][u]
            oaccb_ref[rows_b[u], :] = outs[1][u]

    def body(n, carry):
        back(n, front(n))
        return carry

    lax.fori_loop(0, ngroups, body, 0)

    def fin_body(n, carry):
        rs = pl.ds(pl.multiple_of(n * FIN_ROWS, FIN_ROWS), FIN_ROWS)
        finish(rs, oaccf_ref[rs, :] + oaccb_ref[rs, :])
        return carry

    lax.fori_loop(0, seq // FIN_ROWS, fin_body, 0)


def _hgrn_kernel(q_ref, zf_ref, zb_ref, i_ref, gate_ref, lb_ref, ng_ref,
                 mallf_ref, maskf_ref, mallb_ref, maskb_ref, o_ref, *scratch,
                 layer_idx, n_even, seq):
    lbp = lb_ref[...].astype(F32)
    rows = [[lbp[l * 2 + d:l * 2 + d + 1] for l in range(n_even)] for d in range(2)]
    lower = []
    for d in range(2):
        mx = functools.reduce(jnp.maximum, rows[d])
        ex = [jnp.exp(r - mx) for r in rows[d]]
        tot = functools.reduce(lambda a, b: a + b, ex)
        lbs = [e / tot for e in ex]
        cum = functools.reduce(lambda a, b: a + b, lbs[:layer_idx + 1])
        lower.append(cum - lbs[0])

    def prep(rs, z_ref, lb):
        sg = jax.nn.sigmoid(z_ref[0, rs, :].astype(F32))
        f = lb + (1.0 - lb) * sg
        g = jnp.log(jnp.maximum(f, MIN_GATE))
        k = (1.0 - lb) * (1.0 - sg)
        q = _silu(q_ref[0, rs, :].astype(F32))
        return q.astype(BF16), k.astype(BF16), i_ref[0, rs, :].astype(BF16), g

    def finish(rs, o):
        y = _group_norm_gate(o, ng_ref[...], gate_ref[0, rs, :].astype(F32))
        o_ref[0, rs, :] = y.astype(o_ref.dtype)

    _scan_loop(lambda rs: prep(rs, zf_ref, lower[0]), lambda rs: prep(rs, zb_ref, lower[1]),
               finish, (mallf_ref, maskf_ref, mallb_ref, maskb_ref, *scratch), seq)


def _scan_scratch(t, dk, dv):
    return [pltpu.VMEM((t, dv), F32), pltpu.VMEM((t, dv), F32),
            pltpu.VMEM((dv, dk), F32), pltpu.VMEM((dv, dk), F32)]


def _const_spec(a):
    nd = a.ndim
    return pl.BlockSpec(a.shape, lambda i, j: (0,) * nd)


def _hgrn(proj, lb_param, norm_g, layer_idx, consts):
    b, t, _ = proj.shape
    n_even = lb_param.shape[0]
    hd = HEAD_DIM_B
    col = lambda off: pl.BlockSpec((1, t, hd), lambda i, j: (i, 0, off // hd + j))
    return pl.pallas_call(
        functools.partial(_hgrn_kernel, layer_idx=layer_idx, n_even=n_even, seq=t),
        grid=(b, N_HEADS_B),
        in_specs=[col(E_QB), col(E_ZF), col(E_ZB), col(E_IB), col(E_GB),
                  pl.BlockSpec((n_even * 2, hd), lambda i, j: (0, j)),
                  pl.BlockSpec((1, hd), lambda i, j: (0, j)),
                  *[_const_spec(a) for a in consts]],
        out_specs=pl.BlockSpec((1, t, hd), lambda i, j: (i, 0, j)),
        out_shape=jax.ShapeDtypeStruct((b, t, W_B), BF16),
        scratch_shapes=_scan_scratch(t, hd, hd),
        compiler_params=_params(("parallel", "parallel")),
        name="hgrn",
    )(proj, proj, proj, proj, proj, lb_param.reshape(n_even * 2, W_B),
      norm_g.reshape(1, W_B), *consts)


def _gla_kernel(q_ref, k_ref, v_ref, gate_ref, r_ref, wuh_ref, wul_ref, bias_ref, ng_ref,
                mallf_ref, maskf_ref, mallb_ref, maskb_ref, o_ref, gf_ref, gb_ref, *scratch, seq):
    def gate_body(n, carry):
        rs = pl.ds(pl.multiple_of(n * FIN_ROWS, FIN_ROWS), FIN_ROWS)
        r = r_ref[0, rs, :].astype(F32)
        r_hi = r.astype(BF16)
        r_lo = (r - r_hi.astype(F32)).astype(BF16)
        x = (_dot(r_hi, wuh_ref[...]) + _dot(r_lo, wuh_ref[...]) + _dot(r_hi, wul_ref[...])
             + bias_ref[...])
        g = (jnp.minimum(x, 0.0) - jnp.log(1.0 + jnp.exp(-jnp.abs(x)))) * (1.0 / GATE_TEMP)
        gf_ref[rs, :] = g[:, :DK_C]
        gb_ref[rs, :] = g[:, DK_C:]
        return carry

    lax.fori_loop(0, seq // FIN_ROWS, gate_body, 0)

    def prep(rs, g_ref):
        q = (q_ref[0, rs, :].astype(F32) * (DK_C ** -0.5)).astype(BF16)
        return q, k_ref[0, rs, :].astype(BF16), v_ref[0, rs, :].astype(BF16), g_ref[rs, :]

    def finish(rs, o):
        y = _group_norm_gate(o, ng_ref[...], gate_ref[0, rs, :].astype(F32))
        o_ref[0, rs, :] = y.astype(o_ref.dtype)

    _scan_loop(lambda rs: prep(rs, gf_ref), lambda rs: prep(rs, gb_ref), finish,
               (mallf_ref, maskf_ref, mallb_ref, maskb_ref, *scratch), seq)


def _gla(proj, w_up, b_gate, norm_g, consts):
    b, t, _ = proj.shape
    dk, dv, nh = DK_C, DV_C, N_HEADS_C
    wuf = jnp.zeros((LANE, WK_C), F32).at[0:GATE_RANK].set(w_up[0])
    wub = jnp.zeros((LANE, WK_C), F32).at[GATE_RANK:2 * GATE_RANK].set(w_up[1])
    wcat = jnp.concatenate([wuf.reshape(LANE, nh, dk), wub.reshape(LANE, nh, dk)],
                           axis=2).reshape(LANE, 2 * WK_C)
    w_hi = wcat.astype(BF16)
    w_lo = (wcat - w_hi.astype(F32)).astype(BF16)
    bcat = jnp.concatenate([b_gate[0].reshape(nh, dk), b_gate[1].reshape(nh, dk)],
                           axis=1).reshape(1, 2 * WK_C)
    colk = lambda off: pl.BlockSpec((1, t, dk), lambda i, j: (i, 0, off // dk + j))
    colv = lambda off: pl.BlockSpec((1, t, dv), lambda i, j: (i, 0, off // dv + j))
    return pl.pallas_call(
        functools.partial(_gla_kernel, seq=t),
        grid=(b, nh),
        in_specs=[colk(O_QC), colk(O_KC), colv(O_VC), colv(O_GC),
                  pl.BlockSpec((1, t, LANE), lambda i, j: (i, 0, O_R // LANE)),
                  pl.BlockSpec((LANE, 2 * dk), lambda i, j: (0, j)),
                  pl.BlockSpec((LANE, 2 * dk), lambda i, j: (0, j)),
                  pl.BlockSpec((1, 2 * dk), lambda i, j: (0, j)),
                  pl.BlockSpec((1, dv), lambda i, j: (0, j)),
                  *[_const_spec(a) for a in consts]],
        out_specs=pl.BlockSpec((1, t, dv), lambda i, j: (i, 0, j)),
        out_shape=jax.ShapeDtypeStruct((b, t, WV_C), BF16),
        scratch_shapes=[pltpu.VMEM((t, dk), F32), pltpu.VMEM((t, dk), F32),
                        *_scan_scratch(t, dk, dv)],
        compiler_params=_params(("parallel", "parallel")),
        name="gla",
    )(proj, proj, proj, proj, proj, w_hi, w_lo, bcat, norm_g.reshape(1, WV_C), *consts)


def _even_w_in(w):
    w = w.astype(BF16)
    return jnp.concatenate([w[:, 0:512], w[:, 768:4864], w[:, 512:768]], axis=1)


def _odd_w_in(w):
    w = w.astype(BF16)
    pad = jnp.zeros((w.shape[0], ODD_IN_PAD - 4128), BF16)
    return jnp.concatenate([w[:, 0:3072], w[:, 3104:4128], w[:, 3072:3104], pad], axis=1)


def kernel(x, mem, norm_even, w_in_even, sink, lb_param, hgrn_norm, w_out_even, norm_odd,
           w_in_odd, w_gate_up, b_gate, gla_norm, w_out_odd, mem_norm, w_mem_kv, final_norm):
    b, t, d = x.shape
    consts = (*_scan_constants(False), *_scan_constants(True))
    win_bias = _win_bias()
    kv_all = _memkv(mem, mem_norm, w_mem_kv.astype(BF16))
    x2 = x.reshape(b * t, d)
    for l in range(DEPTH):
        i = l // 2
        g_final = final_norm if l == DEPTH - 1 else None
        if l % 2 == 0:
            proj = _inproj(x2, norm_even[i], _even_w_in(w_in_even[i])).reshape(b, t, EVEN_IN)
            a = _winattn(proj, sink[i], win_bias)
            o = _hgrn(proj, lb_param, hgrn_norm[i], i, consts)
            mo = _memattn(proj, kv_all, l, E_QM, E_GM)
            parts = [a.reshape(b * t, W_A), o.reshape(b * t, W_B), mo.reshape(b * t, W_M)]
            x2 = _outproj(x2, parts, w_out_even[i].astype(BF16), g_final)
        else:
            proj = _inproj(x2, norm_odd[i], _odd_w_in(w_in_odd[i])).reshape(b, t, ODD_IN_PAD)
            o = _gla(proj, w_gate_up[i], b_gate[i], gla_norm[i], consts)
            mo = _memattn(proj, kv_all, l, O_QM, O_GM)
            parts = [o.reshape(b * t, WV_C), mo.reshape(b * t, W_M)]
            x2 = _outproj(x2, parts, w_out_odd[i].astype(BF16), g_final)
    return x2.reshape(b, t, d)
```

```python
import functools

import numpy as np
import jax
import jax.numpy as jnp
from jax import lax
from jax.experimental import pallas as pl
from jax.experimental.pallas import tpu as pltpu

F32 = jnp.float32
BF16 = jnp.bfloat16

D_MODEL = 1024
DEPTH = 4
N_Q_A, N_KV_A, HEAD_DIM_A = 8, 2, 64
W_A, W_KV_A = 512, 128
WINDOW = 128
BLOCK = 128
N_HEADS_B, HEAD_DIM_B, W_B = 4, 128, 512
N_HEADS_C, DK_C, DV_C = 4, 128, 256
WK_C, WV_C = 512, 1024
GATE_RANK = 16
GATE_TEMP = 16.0
N_MEM, N_HEADS_M, HEAD_DIM_M, W_M = 256, 4, 128, 512
EPS = 1e-6
MASK_VALUE = -1e30
MIN_GATE = 1e-30
LOG2E = 1.4426950408889634

EVEN_IN = 4864
E_QA, E_GA, E_QB, E_ZF, E_ZB, E_IB, E_GB, E_QM, E_GM, E_KA, E_VA = (
    0, 512, 1024, 1536, 2048, 2560, 3072, 3584, 4096, 4608, 4736)
ODD_IN_PAD = 4352
O_QC, O_KC, O_VC, O_GC, O_QM, O_GM, O_R = 0, 512, 1024, 2048, 3072, 3584, 4096

LANE = 128
PROJ_DTYPE = F32
PROJ_TM = 256
PROJ_TN = 256
CHUNK = 64
N_LEVELS = 6
LEVEL_PAIRS = ((0, 1), (2, 3), (4, 5), (6, None))
MXU_SKEW = 2
MXU_LEVELS = 3
SCAN_UNROLL = 8
FIN_ROWS = 256
MEM_TQ = 512
VMEM_LIMIT = 56 * 1024 * 1024

_NT = (((1,), (1,)), ((), ()))
_TN = (((0,), (0,)), ((), ()))


def _dot(a, b):
    return jnp.dot(a, b, preferred_element_type=F32)


def _dot_nt(a, b):
    return lax.dot_general(a, b, _NT, preferred_element_type=F32)


def _dot_tn(a, b):
    return lax.dot_general(a, b, _TN, preferred_element_type=F32)


def _silu(x):
    return x * jax.nn.sigmoid(x)


def _params(sem):
    return pltpu.CompilerParams(dimension_semantics=sem, vmem_limit_bytes=VMEM_LIMIT)


def _boundary_kernel(*refs, n_parts, final, has_in):
    refs = list(refs)
    x_ref = refs.pop(0)
    parts = [refs.pop(0) for _ in range(n_parts)]
    w_out_ref = refs.pop(0) if n_parts else None
    gf_ref = refs.pop(0) if final else None
    g_ref, w_in_ref = (refs.pop(0), refs.pop(0)) if has_in else (None, None)
    xo_ref = refs.pop(0) if n_parts else None
    proj_ref = refs.pop(0) if has_in else None

    x = x_ref[...]
    off = 0
    for p in parts:
        wp = p.shape[1]
        x = x + _dot(p[...], w_out_ref[off:off + wp, :])
        off += wp
    if final:
        ms = jnp.mean(x * x, axis=-1, keepdims=True)
        x = x * lax.rsqrt(ms + EPS) * gf_ref[...]
    if n_parts:
        xo_ref[...] = x
    if has_in:
        ms = jnp.mean(x * x, axis=-1, keepdims=True)
        h = (x * lax.rsqrt(ms + EPS) * g_ref[...]).astype(BF16)
        for c in range(0, proj_ref.shape[1], PROJ_TN):
            proj_ref[:, c:c + PROJ_TN] = _dot(h, w_in_ref[:, c:c + PROJ_TN]).astype(proj_ref.dtype)


def _boundary(x2, parts=(), w_out=None, g_final=None, g_in=None, w_in=None):
    n, d = x2.shape
    final, has_in = g_final is not None, w_in is not None
    row = lambda width: pl.BlockSpec((PROJ_TM, width), lambda i: (i, 0))
    whole = lambda a: pl.BlockSpec(a.shape, lambda i: (0, 0))
    args, in_specs = [x2], [row(d)]
    for p in parts:
        args.append(p)
        in_specs.append(row(p.shape[1]))
    if parts:
        args.append(w_out)
        in_specs.append(whole(w_out))
    if final:
        args.append(g_final.reshape(1, d))
        in_specs.append(whole(args[-1]))
    out_shape, out_specs = [], []
    if parts:
        out_shape.append(jax.ShapeDtypeStruct((n, d), F32))
        out_specs.append(row(d))
    if has_in:
        args += [g_in.reshape(1, d), w_in]
        in_specs += [whole(args[-2]), whole(w_in)]
        out_shape.append(jax.ShapeDtypeStruct((n, w_in.shape[1]), PROJ_DTYPE))
        out_specs.append(row(w_in.shape[1]))
    outs = pl.pallas_call(
        functools.partial(_boundary_kernel, n_parts=len(parts), final=final, has_in=has_in),
        grid=(n // PROJ_TM,),
        in_specs=in_specs,
        out_specs=out_specs,
        out_shape=out_shape,
        compiler_params=_params(("parallel",)),
        name="boundary",
    )(*args)
    outs = list(outs)
    return (outs.pop(0) if parts else None), (outs.pop(0) if has_in else None)


def _memkv_kernel(m_ref, g_ref, w_ref, o_ref):
    x = m_ref[0]
    ms = jnp.mean(x * x, axis=-1, keepdims=True)
    h = (x * lax.rsqrt(ms + EPS) * g_ref[...]).astype(BF16)
    o_ref[0, 0] = _dot(h, w_ref[0]).astype(o_ref.dtype)


def _memkv(mem, g, w_kv):
    b, nm, d = mem.shape
    depth, _, wkv = w_kv.shape
    return pl.pallas_call(
        _memkv_kernel,
        grid=(depth, b),
        in_specs=[
            pl.BlockSpec((1, nm, d), lambda l, i: (i, 0, 0)),
            pl.BlockSpec((1, d), lambda l, i: (0, 0)),
            pl.BlockSpec((1, d, wkv), lambda l, i: (l, 0, 0)),
        ],
        out_specs=pl.BlockSpec((1, 1, nm, wkv), lambda l, i: (l, i, 0, 0)),
        out_shape=jax.ShapeDtypeStruct((depth, b, nm, wkv), BF16),
        compiler_params=_params(("parallel", "parallel")),
        name="memkv",
    )(mem, g.reshape(1, d), w_kv)


def _memattn_kernel(q_ref, g_ref, kv_ref, o_ref):
    q = q_ref[0].astype(F32) * (HEAD_DIM_M ** -0.5)
    outs = []
    for h in range(N_HEADS_M):
        sl = slice(h * HEAD_DIM_M, (h + 1) * HEAD_DIM_M)
        qh = q[:, sl].astype(BF16)
        kh = kv_ref[0, 0, :, h * HEAD_DIM_M:(h + 1) * HEAD_DIM_M]
        vh = kv_ref[0, 0, :, W_M + h * HEAD_DIM_M:W_M + (h + 1) * HEAD_DIM_M]
        s = _dot_nt(qh, kh)
        m = jnp.max(s, axis=-1, keepdims=True)
        p = jnp.exp(s - m)
        l = jnp.sum(p, axis=-1, keepdims=True)
        outs.append(_dot(p.astype(BF16), vh) / l)
    o = jnp.concatenate(outs, axis=-1)
    o_ref[0] = (o * _silu(g_ref[0].astype(F32))).astype(o_ref.dtype)


def _memattn(proj, kv, layer, q_off, g_off):
    b, t, _ = proj.shape
    qb, gb = q_off // W_M, g_off // W_M
    return pl.pallas_call(
        _memattn_kernel,
        grid=(b, t // MEM_TQ),
        in_specs=[
            pl.BlockSpec((1, MEM_TQ, W_M), lambda i, j: (i, j, qb)),
            pl.BlockSpec((1, MEM_TQ, W_M), lambda i, j: (i, j, gb)),
            pl.BlockSpec((1, 1, N_MEM, 2 * W_M), lambda i, j: (layer, i, 0, 0)),
        ],
        out_specs=pl.BlockSpec((1, MEM_TQ, W_M), lambda i, j: (i, j, 0)),
        out_shape=jax.ShapeDtypeStruct((b, t, W_M), BF16),
        compiler_params=_params(("parallel", "parallel")),
        name="memattn",
    )(proj, proj, kv)


def _win_bias():
    i = np.arange(BLOCK)[:, None]
    j = np.arange(BLOCK)[None, :]
    out = np.empty((3, N_Q_A, 3, BLOCK, BLOCK), np.float32)
    for e in range(3):
        for h in range(N_Q_A):
            slope = 2.0 ** (-8.0 * (h + 1) / N_Q_A)
            for blk in range(3):
                dist = np.abs(i - j - (blk - 1) * BLOCK)
                valid = dist <= WINDOW
                if (e == 0 and blk == 0) or (e == 2 and blk == 2):
                    valid = np.zeros_like(valid)
                out[e, h, blk] = np.where(valid, -slope * dist * LOG2E, MASK_VALUE)
    return jnp.asarray(out)


def _winattn_kernel(sink_ref, q_ref, g_ref, k_ref, v_ref, bias_ref, o_ref, *, seq):
    c = pl.program_id(1)
    nb = seq // BLOCK
    hd = HEAD_DIM_A
    lo = lax.broadcasted_iota(jnp.int32, (BLOCK, LANE), 1) < hd
    q = (q_ref[0].astype(F32) * (hd ** -0.5 * LOG2E)).astype(BF16)
    kvar, vvar = [], []
    for off in (-1, 0, 1):
        start = pl.multiple_of(jnp.clip(c + off, 0, nb - 1) * BLOCK, BLOCK)
        kb = k_ref[0, pl.ds(start, BLOCK), :].astype(F32)
        vb = v_ref[0, pl.ds(start, BLOCK), :].astype(F32)
        kr = pltpu.roll(kb, hd, axis=1)
        vr = pltpu.roll(vb, hd, axis=1)
        kvar.append({(0, 0): jnp.where(lo, kb, 0.0).astype(BF16), (0, 1): jnp.where(lo, 0.0, kr).astype(BF16),
                     (1, 0): jnp.where(lo, kr, 0.0).astype(BF16), (1, 1): jnp.where(lo, 0.0, kb).astype(BF16)})
        vvar.append({(0, 0): jnp.where(lo, vb, 1.0).astype(BF16), (0, 1): jnp.where(lo, 1.0, vr).astype(BF16),
                     (1, 0): jnp.where(lo, vr, 1.0).astype(BF16), (1, 1): jnp.where(lo, 1.0, vb).astype(BF16)})
    groups = [(n, par) for n in range(N_KV_A) for par in range(2)]
    s = {}
    for n, par in groups:
        lhs = jnp.concatenate([q[:, (2 * n) * LANE:(2 * n + 1) * LANE],
                               q[:, (2 * n + 1) * LANE:(2 * n + 2) * LANE]], axis=0)
        for blk in range(3):
            s[n, par, blk] = _dot_nt(lhs, kvar[blk][n, par])
    p, sink_term = {}, {}
    for n, par in groups:
        for r in range(2):
            h = 4 * n + 2 * r + par
            sink2 = sink_ref[h] * LOG2E
            sb = [s[n, par, blk][r * BLOCK:(r + 1) * BLOCK] + bias_ref[0, h, blk] for blk in range(3)]
            m = jnp.max(jnp.maximum(jnp.maximum(sb[0], sb[1]), sb[2]), axis=-1, keepdims=True)
            m = jnp.maximum(m, sink2)
            p[h] = [jnp.exp2(x - m).astype(BF16) for x in sb]
            sink_term[h] = jnp.exp2(sink2 - m)
    res = {}
    for n, par in groups:
        acc = None
        for blk in range(3):
            lhs = jnp.concatenate([p[4 * n + par][blk], p[4 * n + 2 + par][blk]], axis=0)
            d = _dot(lhs, vvar[blk][n, par])
            acc = d if acc is None else acc + d
        res[n, par] = acc
    for j in range(N_Q_A // 2):
        n, r = j // 2, j % 2
        r_even = res[n, 0][r * BLOCK:(r + 1) * BLOCK]
        r_odd = res[n, 1][r * BLOCK:(r + 1) * BLOCK]
        vals = jnp.where(lo, r_even, r_odd)
        sums = pltpu.roll(jnp.where(lo, r_odd, r_even), hd, axis=1)
        denom = sums + jnp.where(lo, sink_term[2 * j], sink_term[2 * j + 1])
        gate = g_ref[0, :, j * LANE:(j + 1) * LANE].astype(F32)
        o_ref[0, :, j * LANE:(j + 1) * LANE] = (vals / denom * _silu(gate)).astype(o_ref.dtype)


def _winattn(proj, sink, bias):
    b, t, _ = proj.shape
    nb = t // BLOCK
    assert nb >= 2
    edge = lambda i, j: (jnp.where(j == 0, 0, jnp.where(j == nb - 1, 2, 1)), 0, 0, 0, 0)
    return pl.pallas_call(
        functools.partial(_winattn_kernel, seq=t),
        grid=(b, nb),
        in_specs=[
            pl.BlockSpec(memory_space=pltpu.SMEM),
            pl.BlockSpec((1, BLOCK, W_A), lambda i, j: (i, j, E_QA // W_A)),
            pl.BlockSpec((1, BLOCK, W_A), lambda i, j: (i, j, E_GA // W_A)),
            pl.BlockSpec((1, t, W_KV_A), lambda i, j: (i, 0, E_KA // W_KV_A)),
            pl.BlockSpec((1, t, W_KV_A), lambda i, j: (i, 0, E_VA // W_KV_A)),
            pl.BlockSpec((1, N_Q_A, 3, BLOCK, BLOCK), edge),
        ],
        out_specs=pl.BlockSpec((1, BLOCK, W_A), lambda i, j: (i, j, 0)),
        out_shape=jax.ShapeDtypeStruct((b, t, W_A), BF16),
        compiler_params=_params(("parallel", "arbitrary")),
        name="winattn",
    )(sink, proj, proj, proj, proj, bias)


def _scan_constants(rev):
    c = CHUNK
    t = np.arange(c)[:, None]
    u = np.arange(c)[None, :]
    mats = []
    mats.append((u >= t) if rev else (u <= t))
    masks = [t == u]
    for lvl in range(1, N_LEVELS + 1):
        blk, half = 2 ** lvl, 2 ** (lvl - 1)
        m = (t // blk) * blk + half
        if rev:
            mat = np.where(t < m, (u >= t) & (u < m), (u >= m) & (u < t))
            mask = (t // blk == u // blk) & (t % blk < half) & (u % blk >= half)
        else:
            mat = np.where(t >= m, (u >= m) & (u <= t), (u > t) & (u < m))
            mask = (t // blk == u // blk) & (t % blk >= half) & (u % blk < half)
        if lvl <= MXU_LEVELS:
            mats.append(mat)
        masks.append(mask)
    mall = np.concatenate(mats, axis=0).astype(np.float32)
    mall2 = np.concatenate([mall, mall], axis=1)
    none = np.zeros((c, c), bool)
    masks2 = np.stack([np.concatenate([masks[la], none if lb is None else masks[lb]], axis=1)
                       for la, lb in LEVEL_PAIRS]).astype(np.float32)
    return jnp.asarray(mall2, dtype=BF16), jnp.asarray(masks2)


def _level_exponent(b, lvl, rev):
    blk, half = 2 ** lvl, 2 ** (lvl - 1)
    parts = []
    for s in range(0, b.shape[0], blk):
        m = s + half
        lower, upper = b[s:m], b[m:s + blk]
        if rev:
            r = b[m:m + 1]
            parts += [lower - r, r - upper]
        else:
            r = b[m - 1:m]
            parts += [r - lower, upper - r]
    return jnp.concatenate(parts, axis=0)


def _split2(g):
    g1 = g.astype(BF16)
    g2 = (g - g1.astype(F32)).astype(BF16)
    return jnp.concatenate([g1, g2], axis=0)


def _scan_front(dirs):
    c = CHUNK
    e_tots, es = {}, {}
    for di, (chunks, mall2_ref, _, rev) in enumerate(dirs):
        g12 = jnp.concatenate([_split2(g * LOG2E) for (_, _, _, g) in chunks], axis=1)
        ex = _dot(mall2_ref[...], g12)
        for u, (q, _, _, _) in enumerate(chunks):
            dk = q.shape[1]
            exu = ex[:, u * dk:(u + 1) * dk]
            b = exu[0:c]
            b_tot = b[0:1] if rev else b[c - 1:c]
            full = jnp.concatenate(
                [exu] + [_level_exponent(b, lvl, rev) for lvl in range(MXU_LEVELS + 1, N_LEVELS + 1)]
                + [b_tot - b], axis=0)
            e_tots[di, u] = jnp.exp2(b_tot)
            es[di, u] = jnp.exp2(full).astype(BF16)

    def e_rows(di, u, blk):
        return es[di, u][blk * c:(blk + 1) * c]

    o_intra, w_inc, a2s = {}, {}, {}
    chains = [(di, u) for u in range(SCAN_UNROLL) for di in range(len(dirs))]

    def scores(di, u):
        q, k, v, _ = dirs[di][0][u]
        mask_ref = dirs[di][2]
        zeros = jnp.zeros_like(k)
        a2 = None
        for pi, (la, lb) in enumerate(LEVEL_PAIRS):
            qa, ka = (q, k) if la == 0 else (q * e_rows(di, u, la), k * e_rows(di, u, la))
            if lb is None:
                lhs, rhs = qa, jnp.concatenate([ka, zeros], axis=0)
            else:
                e_l = e_rows(di, u, lb)
                lhs = jnp.concatenate([qa, q * e_l], axis=1)
                rhs = jnp.concatenate([jnp.concatenate([ka, zeros], axis=1),
                                       jnp.concatenate([zeros, k * e_l], axis=1)], axis=0)
            p = mask_ref[pi] * _dot_nt(lhs, rhs)
            a2 = p if a2 is None else a2 + p
        a2s[di, u] = a2
        w_inc[di, u] = _dot_tn(v, k * e_rows(di, u, N_LEVELS + 1))

    def intra(di, u):
        v = dirs[di][0][u][2]
        o_intra[di, u] = _dot(a2s[di, u].astype(BF16), jnp.concatenate([v, v], axis=0))

    for i in range(len(chains) + MXU_SKEW):
        if i < len(chains):
            scores(*chains[i])
        if i >= MXU_SKEW:
            intra(*chains[i - MXU_SKEW])
    return [[(dirs[di][0][u][0] * e_rows(di, u, 0), o_intra[di, u], w_inc[di, u], e_tots[di, u])
             for u in range(SCAN_UNROLL)] for di in range(len(dirs))]


def _scan_back(pend, st_refs):
    sts = [r[...] for r in st_refs]
    outs = [[None] * SCAN_UNROLL for _ in st_refs]
    for u in range(SCAN_UNROLL):
        for di in range(len(st_refs)):
            qe, o_intra, w_inc, e_tot = pend[di][u]
            st = sts[di]
            outs[di][u] = o_intra + _dot_nt(qe, st.astype(BF16))
            sts[di] = st * e_tot + w_inc
    for r, st in zip(st_refs, sts):
        r[...] = st
    return outs


def _group_norm_gate(o, norm_g, gate):
    ms = jnp.mean(o * o, axis=-1, keepdims=True)
    return o * lax.rsqrt(ms + EPS) * norm_g * _silu(gate)


def _scan_loop(prep_fwd, prep_bwd, finish, refs, seq):
    mallf_ref, maskf_ref, mallb_ref, maskb_ref, oaccf_ref, oaccb_ref, stf_ref, stb_ref = refs
    nch = seq // CHUNK
    stf_ref[...] = jnp.zeros_like(stf_ref)
    stb_ref[...] = jnp.zeros_like(stb_ref)

    ngroups = nch // SCAN_UNROLL

    def rows(n):
        rows_f = [pl.ds(pl.multiple_of((n * SCAN_UNROLL + u) * CHUNK, CHUNK), CHUNK)
                  for u in range(SCAN_UNROLL)]
        rows_b = [pl.ds(pl.multiple_of((nch - 1 - n * SCAN_UNROLL - u) * CHUNK, CHUNK), CHUNK)
                  for u in range(SCAN_UNROLL)]
        return rows_f, rows_b

    def front(n):
        rows_f, rows_b = rows(n)
        return _scan_front([
            ([prep_fwd(rs) for rs in rows_f], mallf_ref, maskf_ref, False),
            ([prep_bwd(rs) for rs in rows_b], mallb_ref, maskb_ref, True)])

    def back(n, pend):
        rows_f, rows_b = rows(n)
        outs = _scan_back(pend, (stf_ref, stb_ref))
        for u in range(SCAN_UNROLL):
            oaccf_ref[rows_f[u], :] = outs[0][u]
            oaccb_ref[rows_b[u], :] = outs[1][u]

    def body(n, carry):
        back(n, front(n))
        return carry

    lax.fori_loop(0, ngroups, body, 0)

    def fin_body(n, carry):
        rs = pl.ds(pl.multiple_of(n * FIN_ROWS, FIN_ROWS), FIN_ROWS)
        finish(rs, oaccf_ref[rs, :] + oaccb_ref[rs, :])
        return carry

    lax.fori_loop(0, seq // FIN_ROWS, fin_body, 0)


def _hgrn_kernel(q_ref, zf_ref, zb_ref, i_ref, gate_ref, lb_ref, ng_ref,
                 mallf_ref, maskf_ref, mallb_ref, maskb_ref, o_ref, *scratch,
                 layer_idx, n_even, seq):
    lbp = lb_ref[...].astype(F32)
    rows = [[lbp[l * 2 + d:l * 2 + d + 1] for l in range(n_even)] for d in range(2)]
    lower = []
    for d in range(2):
        mx = functools.reduce(jnp.maximum, rows[d])
        ex = [jnp.exp(r - mx) for r in rows[d]]
        tot = functools.reduce(lambda a, b: a + b, ex)
        lbs = [e / tot for e in ex]
        cum = functools.reduce(lambda a, b: a + b, lbs[:layer_idx + 1])
        lower.append(cum - lbs[0])

    def prep(rs, z_ref, lb):
        sg = jax.nn.sigmoid(z_ref[0, rs, :].astype(F32))
        f = lb + (1.0 - lb) * sg
        g = jnp.log(jnp.maximum(f, MIN_GATE))
        k = (1.0 - lb) * (1.0 - sg)
        q = _silu(q_ref[0, rs, :].astype(F32))
        return q.astype(BF16), k.astype(BF16), i_ref[0, rs, :].astype(BF16), g

    def finish(rs, o):
        y = _group_norm_gate(o, ng_ref[...], gate_ref[0, rs, :].astype(F32))
        o_ref[0, rs, :] = y.astype(o_ref.dtype)

    _scan_loop(lambda rs: prep(rs, zf_ref, lower[0]), lambda rs: prep(rs, zb_ref, lower[1]),
               finish, (mallf_ref, maskf_ref, mallb_ref, maskb_ref, *scratch), seq)


def _scan_scratch(t, dk, dv):
    return [pltpu.VMEM((t, dv), F32), pltpu.VMEM((t, dv), F32),
            pltpu.VMEM((dv, dk), F32), pltpu.VMEM((dv, dk), F32)]


def _const_spec(a):
    nd = a.ndim
    return pl.BlockSpec(a.shape, lambda i, j: (0,) * nd)


def _hgrn(proj, lb_param, norm_g, layer_idx, consts):
    b, t, _ = proj.shape
    n_even = lb_param.shape[0]
    hd = HEAD_DIM_B
    col = lambda off: pl.BlockSpec((1, t, hd), lambda i, j: (i, 0, off // hd + j))
    return pl.pallas_call(
        functools.partial(_hgrn_kernel, layer_idx=layer_idx, n_even=n_even, seq=t),
        grid=(b, N_HEADS_B),
        in_specs=[col(E_QB), col(E_ZF), col(E_ZB), col(E_IB), col(E_GB),
                  pl.BlockSpec((n_even * 2, hd), lambda i, j: (0, j)),
                  pl.BlockSpec((1, hd), lambda i, j: (0, j)),
                  *[_const_spec(a) for a in consts]],
        out_specs=pl.BlockSpec((1, t, hd), lambda i, j: (i, 0, j)),
        out_shape=jax.ShapeDtypeStruct((b, t, W_B), BF16),
        scratch_shapes=_scan_scratch(t, hd, hd),
        compiler_params=_params(("parallel", "parallel")),
        name="hgrn",
    )(proj, proj, proj, proj, proj, lb_param.reshape(n_even * 2, W_B),
      norm_g.reshape(1, W_B), *consts)


def _gla_kernel(q_ref, k_ref, v_ref, gate_ref, r_ref, wuh_ref, wul_ref, bias_ref, ng_ref,
                mallf_ref, maskf_ref, mallb_ref, maskb_ref, o_ref, gf_ref, gb_ref, *scratch, seq):
    def gate_body(n, carry):
        rs = pl.ds(pl.multiple_of(n * FIN_ROWS, FIN_ROWS), FIN_ROWS)
        r = r_ref[0, rs, :].astype(F32)
        r_hi = r.astype(BF16)
        r_lo = (r - r_hi.astype(F32)).astype(BF16)
        x = (_dot(r_hi, wuh_ref[...]) + _dot(r_lo, wuh_ref[...]) + _dot(r_hi, wul_ref[...])
             + bias_ref[...])
        g = (jnp.minimum(x, 0.0) - jnp.log(1.0 + jnp.exp(-jnp.abs(x)))) * (1.0 / GATE_TEMP)
        gf_ref[rs, :] = g[:, :DK_C]
        gb_ref[rs, :] = g[:, DK_C:]
        return carry

    lax.fori_loop(0, seq // FIN_ROWS, gate_body, 0)

    def prep(rs, g_ref):
        q = (q_ref[0, rs, :].astype(F32) * (DK_C ** -0.5)).astype(BF16)
        return q, k_ref[0, rs, :].astype(BF16), v_ref[0, rs, :].astype(BF16), g_ref[rs, :]

    def finish(rs, o):
        y = _group_norm_gate(o, ng_ref[...], gate_ref[0, rs, :].astype(F32))
        o_ref[0, rs, :] = y.astype(o_ref.dtype)

    _scan_loop(lambda rs: prep(rs, gf_ref), lambda rs: prep(rs, gb_ref), finish,
               (mallf_ref, maskf_ref, mallb_ref, maskb_ref, *scratch), seq)


def _gla(proj, w_up, b_gate, norm_g, consts):
    b, t, _ = proj.shape
    dk, dv, nh = DK_C, DV_C, N_HEADS_C
    wuf = jnp.zeros((LANE, WK_C), F32).at[0:GATE_RANK].set(w_up[0])
    wub = jnp.zeros((LANE, WK_C), F32).at[GATE_RANK:2 * GATE_RANK].set(w_up[1])
    wcat = jnp.concatenate([wuf.reshape(LANE, nh, dk), wub.reshape(LANE, nh, dk)],
                           axis=2).reshape(LANE, 2 * WK_C)
    w_hi = wcat.astype(BF16)
    w_lo = (wcat - w_hi.astype(F32)).astype(BF16)
    bcat = jnp.concatenate([b_gate[0].reshape(nh, dk), b_gate[1].reshape(nh, dk)],
                           axis=1).reshape(1, 2 * WK_C)
    colk = lambda off: pl.BlockSpec((1, t, dk), lambda i, j: (i, 0, off // dk + j))
    colv = lambda off: pl.BlockSpec((1, t, dv), lambda i, j: (i, 0, off // dv + j))
    return pl.pallas_call(
        functools.partial(_gla_kernel, seq=t),
        grid=(b, nh),
        in_specs=[colk(O_QC), colk(O_KC), colv(O_VC), colv(O_GC),
                  pl.BlockSpec((1, t, LANE), lambda i, j: (i, 0, O_R // LANE)),
                  pl.BlockSpec((LANE, 2 * dk), lambda i, j: (0, j)),
                  pl.BlockSpec((LANE, 2 * dk), lambda i, j: (0, j)),
                  pl.BlockSpec((1, 2 * dk), lambda i, j: (0, j)),
                  pl.BlockSpec((1, dv), lambda i, j: (0, j)),
                  *[_const_spec(a) for a in consts]],
        out_specs=pl.BlockSpec((1, t, dv), lambda i, j: (i, 0, j)),
        out_shape=jax.ShapeDtypeStruct((b, t, WV_C), BF16),
        scratch_shapes=[pltpu.VMEM((t, dk), F32), pltpu.VMEM((t, dk), F32),
                        *_scan_scratch(t, dk, dv)],
        compiler_params=_params(("parallel", "parallel")),
        name="gla",
    )(proj, proj, proj, proj, proj, w_hi, w_lo, bcat, norm_g.reshape(1, WV_C), *consts)


def _even_w_in(w):
    return jnp.concatenate([w[:, 0:512], w[:, 768:4864], w[:, 512:768]], axis=1).astype(BF16)


def _odd_w_in(w):
    pad = jnp.zeros((w.shape[0], ODD_IN_PAD - 4128), w.dtype)
    return jnp.concatenate([w[:, 0:3072], w[:, 3104:4128], w[:, 3072:3104], pad],
                           axis=1).astype(BF16)


def kernel(x, mem, norm_even, w_in_even, sink, lb_param, hgrn_norm, w_out_even, norm_odd,
           w_in_odd, w_gate_up, b_gate, gla_norm, w_out_odd, mem_norm, w_mem_kv, final_norm):
    b, t, d = x.shape
    consts = (*_scan_constants(False), *_scan_constants(True))
    win_bias = _win_bias()
    kv_all = _memkv(mem, mem_norm, w_mem_kv.astype(BF16))

    def in_weights(l):
        i = l // 2
        if l % 2 == 0:
            return norm_even[i], _even_w_in(w_in_even[i])
        return norm_odd[i], _odd_w_in(w_in_odd[i])

    x2 = x.reshape(b * t, d)
    g_in, w_in = in_weights(0)
    _, proj = _boundary(x2, g_in=g_in, w_in=w_in)
    for l in range(DEPTH):
        i = l // 2
        if l % 2 == 0:
            proj = proj.reshape(b, t, EVEN_IN)
            a = _winattn(proj, sink[i], win_bias)
            o = _hgrn(proj, lb_param, hgrn_norm[i], i, consts)
            mo = _memattn(proj, kv_all, l, E_QM, E_GM)
            parts = [a.reshape(b * t, W_A), o.reshape(b * t, W_B), mo.reshape(b * t, W_M)]
            w_out = w_out_even[i].astype(BF16)
        else:
            proj = proj.reshape(b, t, ODD_IN_PAD)
            o = _gla(proj, w_gate_up[i], b_gate[i], gla_norm[i], consts)
            mo = _memattn(proj, kv_all, l, O_QM, O_GM)
            parts = [o.reshape(b * t, WV_C), mo.reshape(b * t, W_M)]
            w_out = w_out_odd[i].astype(BF16)
        if l == DEPTH - 1:
            x2, _ = _boundary(x2, parts, w_out, g_final=final_norm)
        else:
            g_in, w_in = in_weights(l + 1)
            x2, proj = _boundary(x2, parts, w_out, g_in=g_in, w_in=w_in)
    return x2.reshape(b, t, d)
```

```python
import functools

import numpy as np
import jax
import jax.numpy as jnp
from jax import lax
from jax.experimental import pallas as pl
from jax.experimental.pallas import tpu as pltpu

F32 = jnp.float32
BF16 = jnp.bfloat16

D_MODEL = 1024
DEPTH = 4
N_Q_A, N_KV_A, HEAD_DIM_A = 8, 2, 64
W_A, W_KV_A = 512, 128
WINDOW = 128
BLOCK = 128
N_HEADS_B, HEAD_DIM_B, W_B = 4, 128, 512
N_HEADS_C, DK_C, DV_C = 4, 128, 256
WK_C, WV_C = 512, 1024
GATE_RANK = 16
GATE_TEMP = 16.0
N_MEM, N_HEADS_M, HEAD_DIM_M, W_M = 256, 4, 128, 512
EPS = 1e-6
MASK_VALUE = -1e30
MIN_GATE = 1e-30
LOG2E = 1.4426950408889634

EVEN_IN = 4864
E_QA, E_GA, E_QB, E_ZF, E_ZB, E_IB, E_GB, E_QM, E_GM, E_KA, E_VA = (
    0, 512, 1024, 1536, 2048, 2560, 3072, 3584, 4096, 4608, 4736)
ODD_IN_PAD = 4352
O_QC, O_KC, O_VC, O_GC, O_QM, O_GM, O_R = 0, 512, 1024, 2048, 3072, 3584, 4096

LANE = 128
PROJ_DTYPE = BF16
PROJ_TM = 256
PROJ_TN = 256
CHUNK = 64
N_LEVELS = 6
LEVEL_PAIRS = ((0, 1), (2, 3), (4, 5), (6, None))
MXU_SKEW = 2
MXU_LEVELS = 3
SCAN_UNROLL = 8
FIN_ROWS = 256
MEM_TQ = 512
VMEM_LIMIT = 56 * 1024 * 1024

_NT = (((1,), (1,)), ((), ()))
_TN = (((0,), (0,)), ((), ()))


def _dot(a, b):
    return jnp.dot(a, b, preferred_element_type=F32)


def _dot_nt(a, b):
    return lax.dot_general(a, b, _NT, preferred_element_type=F32)


def _dot_tn(a, b):
    return lax.dot_general(a, b, _TN, preferred_element_type=F32)


def _silu(x):
    return x * jax.nn.sigmoid(x)


def _params(sem):
    return pltpu.CompilerParams(dimension_semantics=sem, vmem_limit_bytes=VMEM_LIMIT)


def _boundary_kernel(*refs, n_parts, final, has_in):
    refs = list(refs)
    x_ref = refs.pop(0)
    parts = [refs.pop(0) for _ in range(n_parts)]
    w_out_ref = refs.pop(0) if n_parts else None
    gf_ref = refs.pop(0) if final else None
    g_ref, w_in_ref = (refs.pop(0), refs.pop(0)) if has_in else (None, None)
    xo_ref = refs.pop(0) if n_parts else None
    proj_ref = refs.pop(0) if has_in else None

    x = x_ref[...]
    off = 0
    for p in parts:
        wp = p.shape[1]
        x = x + _dot(p[...], w_out_ref[off:off + wp, :])
        off += wp
    if final:
        ms = jnp.mean(x * x, axis=-1, keepdims=True)
        x = x * lax.rsqrt(ms + EPS) * gf_ref[...]
    if n_parts:
        xo_ref[...] = x
    if has_in:
        ms = jnp.mean(x * x, axis=-1, keepdims=True)
        h = (x * lax.rsqrt(ms + EPS) * g_ref[...]).astype(BF16)
        for c in range(0, proj_ref.shape[1], PROJ_TN):
            proj_ref[:, c:c + PROJ_TN] = _dot(h, w_in_ref[:, c:c + PROJ_TN]).astype(proj_ref.dtype)


def _boundary(x2, parts=(), w_out=None, g_final=None, g_in=None, w_in=None):
    n, d = x2.shape
    final, has_in = g_final is not None, w_in is not None
    row = lambda width: pl.BlockSpec((PROJ_TM, width), lambda i: (i, 0))
    whole = lambda a: pl.BlockSpec(a.shape, lambda i: (0, 0))
    layer = lambda a, l: pl.BlockSpec((None,) + a.shape[1:], lambda i: (l, 0, 0))
    args, in_specs = [x2], [row(d)]
    for p in parts:
        args.append(p)
        in_specs.append(row(p.shape[1]))
    if parts:
        args.append(w_out[0])
        in_specs.append(layer(*w_out))
    if final:
        args.append(g_final.reshape(1, d))
        in_specs.append(whole(args[-1]))
    out_shape, out_specs = [], []
    if parts:
        out_shape.append(jax.ShapeDtypeStruct((n, d), F32))
        out_specs.append(row(d))
    if has_in:
        n_out = w_in[0].shape[2]
        args += [g_in.reshape(1, d), w_in[0]]
        in_specs += [whole(args[-2]), layer(*w_in)]
        out_shape.append(jax.ShapeDtypeStruct((n, n_out), PROJ_DTYPE))
        out_specs.append(row(n_out))
    outs = pl.pallas_call(
        functools.partial(_boundary_kernel, n_parts=len(parts), final=final, has_in=has_in),
        grid=(n // PROJ_TM,),
        in_specs=in_specs,
        out_specs=out_specs,
        out_shape=out_shape,
        compiler_params=_params(("parallel",)),
        name="boundary",
    )(*args)
    outs = list(outs)
    return (outs.pop(0) if parts else None), (outs.pop(0) if has_in else None)


def _memkv_kernel(m_ref, g_ref, w_ref, o_ref):
    x = m_ref[0]
    ms = jnp.mean(x * x, axis=-1, keepdims=True)
    h = (x * lax.rsqrt(ms + EPS) * g_ref[...]).astype(BF16)
    o_ref[0, 0] = _dot(h, w_ref[0]).astype(o_ref.dtype)


def _memkv(mem, g, w_kv):
    b, nm, d = mem.shape
    depth, _, wkv = w_kv.shape
    return pl.pallas_call(
        _memkv_kernel,
        grid=(depth, b),
        in_specs=[
            pl.BlockSpec((1, nm, d), lambda l, i: (i, 0, 0)),
            pl.BlockSpec((1, d), lambda l, i: (0, 0)),
            pl.BlockSpec((1, d, wkv), lambda l, i: (l, 0, 0)),
        ],
        out_specs=pl.BlockSpec((1, 1, nm, wkv), lambda l, i: (l, i, 0, 0)),
        out_shape=jax.ShapeDtypeStruct((depth, b, nm, wkv), BF16),
        compiler_params=_params(("parallel", "parallel")),
        name="memkv",
    )(mem, g.reshape(1, d), w_kv)


def _memattn_kernel(q_ref, g_ref, kv_ref, o_ref):
    q = q_ref[0].astype(F32) * (HEAD_DIM_M ** -0.5)
    outs = []
    for h in range(N_HEADS_M):
        sl = slice(h * HEAD_DIM_M, (h + 1) * HEAD_DIM_M)
        qh = q[:, sl].astype(BF16)
        kh = kv_ref[0, 0, :, h * HEAD_DIM_M:(h + 1) * HEAD_DIM_M]
        vh = kv_ref[0, 0, :, W_M + h * HEAD_DIM_M:W_M + (h + 1) * HEAD_DIM_M]
        s = _dot_nt(qh, kh)
        m = jnp.max(s, axis=-1, keepdims=True)
        p = jnp.exp(s - m)
        l = jnp.sum(p, axis=-1, keepdims=True)
        outs.append(_dot(p.astype(BF16), vh) / l)
    o = jnp.concatenate(outs, axis=-1)
    o_ref[0] = (o * _silu(g_ref[0].astype(F32))).astype(o_ref.dtype)


def _memattn(proj, kv, layer, q_off, g_off):
    b, t, _ = proj.shape
    qb, gb = q_off // W_M, g_off // W_M
    return pl.pallas_call(
        _memattn_kernel,
        grid=(b, t // MEM_TQ),
        in_specs=[
            pl.BlockSpec((1, MEM_TQ, W_M), lambda i, j: (i, j, qb)),
            pl.BlockSpec((1, MEM_TQ, W_M), lambda i, j: (i, j, gb)),
            pl.BlockSpec((1, 1, N_MEM, 2 * W_M), lambda i, j: (layer, i, 0, 0)),
        ],
        out_specs=pl.BlockSpec((1, MEM_TQ, W_M), lambda i, j: (i, j, 0)),
        out_shape=jax.ShapeDtypeStruct((b, t, W_M), BF16),
        compiler_params=_params(("parallel", "parallel")),
        name="memattn",
    )(proj, proj, kv)


def _win_bias():
    i = np.arange(BLOCK)[:, None]
    j = np.arange(BLOCK)[None, :]
    out = np.empty((3, N_Q_A, 3, BLOCK, BLOCK), np.float32)
    for e in range(3):
        for h in range(N_Q_A):
            slope = 2.0 ** (-8.0 * (h + 1) / N_Q_A)
            for blk in range(3):
                dist = np.abs(i - j - (blk - 1) * BLOCK)
                valid = dist <= WINDOW
                if (e == 0 and blk == 0) or (e == 2 and blk == 2):
                    valid = np.zeros_like(valid)
                out[e, h, blk] = np.where(valid, -slope * dist * LOG2E, MASK_VALUE)
    return jnp.asarray(out)


def _winattn_kernel(sink_ref, q_ref, g_ref, k_ref, v_ref, bias_ref, o_ref, *, seq):
    c = pl.program_id(1)
    nb = seq // BLOCK
    hd = HEAD_DIM_A
    lo = lax.broadcasted_iota(jnp.int32, (BLOCK, LANE), 1) < hd
    q = (q_ref[0].astype(F32) * (hd ** -0.5 * LOG2E)).astype(BF16)
    kvar, vvar = [], []
    for off in (-1, 0, 1):
        start = pl.multiple_of(jnp.clip(c + off, 0, nb - 1) * BLOCK, BLOCK)
        kb = k_ref[0, pl.ds(start, BLOCK), :].astype(F32)
        vb = v_ref[0, pl.ds(start, BLOCK), :].astype(F32)
        kr = pltpu.roll(kb, hd, axis=1)
        vr = pltpu.roll(vb, hd, axis=1)
        kvar.append({(0, 0): jnp.where(lo, kb, 0.0).astype(BF16), (0, 1): jnp.where(lo, 0.0, kr).astype(BF16),
                     (1, 0): jnp.where(lo, kr, 0.0).astype(BF16), (1, 1): jnp.where(lo, 0.0, kb).astype(BF16)})
        vvar.append({(0, 0): jnp.where(lo, vb, 1.0).astype(BF16), (0, 1): jnp.where(lo, 1.0, vr).astype(BF16),
                     (1, 0): jnp.where(lo, vr, 1.0).astype(BF16), (1, 1): jnp.where(lo, 1.0, vb).astype(BF16)})
    groups = [(n, par) for n in range(N_KV_A) for par in range(2)]
    s = {}
    for n, par in groups:
        lhs = jnp.concatenate([q[:, (2 * n) * LANE:(2 * n + 1) * LANE],
                               q[:, (2 * n + 1) * LANE:(2 * n + 2) * LANE]], axis=0)
        for blk in range(3):
            s[n, par, blk] = _dot_nt(lhs, kvar[blk][n, par])
    p, sink_term = {}, {}
    for n, par in groups:
        for r in range(2):
            h = 4 * n + 2 * r + par
            sink2 = sink_ref[h] * LOG2E
            sb = [s[n, par, blk][r * BLOCK:(r + 1) * BLOCK] + bias_ref[0, h, blk] for blk in range(3)]
            m = jnp.max(jnp.maximum(jnp.maximum(sb[0], sb[1]), sb[2]), axis=-1, keepdims=True)
            m = jnp.maximum(m, sink2)
            p[h] = [jnp.exp2(x - m).astype(BF16) for x in sb]
            sink_term[h] = jnp.exp2(sink2 - m)
    res = {}
    for n, par in groups:
        acc = None
        for blk in range(3):
            lhs = jnp.concatenate([p[4 * n + par][blk], p[4 * n + 2 + par][blk]], axis=0)
            d = _dot(lhs, vvar[blk][n, par])
            acc = d if acc is None else acc + d
        res[n, par] = acc
    for j in range(N_Q_A // 2):
        n, r = j // 2, j % 2
        r_even = res[n, 0][r * BLOCK:(r + 1) * BLOCK]
        r_odd = res[n, 1][r * BLOCK:(r + 1) * BLOCK]
        vals = jnp.where(lo, r_even, r_odd)
        sums = pltpu.roll(jnp.where(lo, r_odd, r_even), hd, axis=1)
        denom = sums + jnp.where(lo, sink_term[2 * j], sink_term[2 * j + 1])
        gate = g_ref[0, :, j * LANE:(j + 1) * LANE].astype(F32)
        o_ref[0, :, j * LANE:(j + 1) * LANE] = (vals / denom * _silu(gate)).astype(o_ref.dtype)


def _winattn(proj, sink, bias):
    b, t, _ = proj.shape
    nb = t // BLOCK
    assert nb >= 2
    edge = lambda i, j: (jnp.where(j == 0, 0, jnp.where(j == nb - 1, 2, 1)), 0, 0, 0, 0)
    return pl.pallas_call(
        functools.partial(_winattn_kernel, seq=t),
        grid=(b, nb),
        in_specs=[
            pl.BlockSpec(memory_space=pltpu.SMEM),
            pl.BlockSpec((1, BLOCK, W_A), lambda i, j: (i, j, E_QA // W_A)),
            pl.BlockSpec((1, BLOCK, W_A), lambda i, j: (i, j, E_GA // W_A)),
            pl.BlockSpec((1, t, W_KV_A), lambda i, j: (i, 0, E_KA // W_KV_A)),
            pl.BlockSpec((1, t, W_KV_A), lambda i, j: (i, 0, E_VA // W_KV_A)),
            pl.BlockSpec((1, N_Q_A, 3, BLOCK, BLOCK), edge),
        ],
        out_specs=pl.BlockSpec((1, BLOCK, W_A), lambda i, j: (i, j, 0)),
        out_shape=jax.ShapeDtypeStruct((b, t, W_A), BF16),
        compiler_params=_params(("parallel", "arbitrary")),
        name="winattn",
    )(sink, proj, proj, proj, proj, bias)


def _scan_constants(rev):
    c = CHUNK
    t = np.arange(c)[:, None]
    u = np.arange(c)[None, :]
    mats = []
    mats.append((u >= t) if rev else (u <= t))
    masks = [t == u]
    for lvl in range(1, N_LEVELS + 1):
        blk, half = 2 ** lvl, 2 ** (lvl - 1)
        m = (t // blk) * blk + half
        if rev:
            mat = np.where(t < m, (u >= t) & (u < m), (u >= m) & (u < t))
            mask = (t // blk == u // blk) & (t % blk < half) & (u % blk >= half)
        else:
            mat = np.where(t >= m, (u >= m) & (u <= t), (u > t) & (u < m))
            mask = (t // blk == u // blk) & (t % blk >= half) & (u % blk < half)
        if lvl <= MXU_LEVELS:
            mats.append(mat)
        masks.append(mask)
    mall = np.concatenate(mats, axis=0).astype(np.float32)
    mall2 = np.concatenate([mall, mall], axis=1)
    none = np.zeros((c, c), bool)
    masks2 = np.stack([np.concatenate([masks[la], none if lb is None else masks[lb]], axis=1)
                       for la, lb in LEVEL_PAIRS]).astype(np.float32)
    return jnp.asarray(mall2, dtype=BF16), jnp.asarray(masks2)


def _level_exponent(b, lvl, rev):
    blk, half = 2 ** lvl, 2 ** (lvl - 1)
    parts = []
    for s in range(0, b.shape[0], blk):
        m = s + half
        lower, upper = b[s:m], b[m:s + blk]
        if rev:
            r = b[m:m + 1]
            parts += [lower - r, r - upper]
        else:
            r = b[m - 1:m]
            parts += [r - lower, upper - r]
    return jnp.concatenate(parts, axis=0)


def _split2(g):
    g1 = g.astype(BF16)
    g2 = (g - g1.astype(F32)).astype(BF16)
    return jnp.concatenate([g1, g2], axis=0)


def _scan_front(dirs):
    c = CHUNK
    e_tots, es = {}, {}
    for di, (chunks, mall2_ref, _, rev) in enumerate(dirs):
        g12 = jnp.concatenate([_split2(g * LOG2E) for (_, _, _, g) in chunks], axis=1)
        ex = _dot(mall2_ref[...], g12)
        for u, (q, _, _, _) in enumerate(chunks):
            dk = q.shape[1]
            exu = ex[:, u * dk:(u + 1) * dk]
            b = exu[0:c]
            b_tot = b[0:1] if rev else b[c - 1:c]
            full = jnp.concatenate(
                [exu] + [_level_exponent(b, lvl, rev) for lvl in range(MXU_LEVELS + 1, N_LEVELS + 1)]
                + [b_tot - b], axis=0)
            e_tots[di, u] = jnp.exp2(b_tot)
            es[di, u] = jnp.exp2(full).astype(BF16)

    def e_rows(di, u, blk):
        return es[di, u][blk * c:(blk + 1) * c]

    o_intra, w_inc, a2s = {}, {}, {}
    chains = [(di, u) for u in range(SCAN_UNROLL) for di in range(len(dirs))]

    def scores(di, u):
        q, k, v, _ = dirs[di][0][u]
        mask_ref = dirs[di][2]
        zeros = jnp.zeros_like(k)
        a2 = None
        for pi, (la, lb) in enumerate(LEVEL_PAIRS):
            qa, ka = (q, k) if la == 0 else (q * e_rows(di, u, la), k * e_rows(di, u, la))
            if lb is None:
                lhs, rhs = qa, jnp.concatenate([ka, zeros], axis=0)
            else:
                e_l = e_rows(di, u, lb)
                lhs = jnp.concatenate([qa, q * e_l], axis=1)
                rhs = jnp.concatenate([jnp.concatenate([ka, zeros], axis=1),
                                       jnp.concatenate([zeros, k * e_l], axis=1)], axis=0)
            p = mask_ref[pi] * _dot_nt(lhs, rhs)
            a2 = p if a2 is None else a2 + p
        a2s[di, u] = a2
        w_inc[di, u] = _dot_tn(v, k * e_rows(di, u, N_LEVELS + 1))

    def intra(di, u):
        v = dirs[di][0][u][2]
        o_intra[di, u] = _dot(a2s[di, u].astype(BF16), jnp.concatenate([v, v], axis=0))

    for i in range(len(chains) + MXU_SKEW):
        if i < len(chains):
            scores(*chains[i])
        if i >= MXU_SKEW:
            intra(*chains[i - MXU_SKEW])
    return [[(dirs[di][0][u][0] * e_rows(di, u, 0), o_intra[di, u], w_inc[di, u], e_tots[di, u])
             for u in range(SCAN_UNROLL)] for di in range(len(dirs))]


def _scan_back(pend, st_refs):
    sts = [r[...] for r in st_refs]
    outs = [[None] * SCAN_UNROLL for _ in st_refs]
    for u in range(SCAN_UNROLL):
        for di in range(len(st_refs)):
            qe, o_intra, w_inc, e_tot = pend[di][u]
            st = sts[di]
            outs[di][u] = o_intra + _dot_nt(qe, st.astype(BF16))
            sts[di] = st * e_tot + w_inc
    for r, st in zip(st_refs, sts):
        r[...] = st
    return outs


def _group_norm_gate(o, norm_g, gate):
    ms = jnp.mean(o * o, axis=-1, keepdims=True)
    return o * lax.rsqrt(ms + EPS) * norm_g * _silu(gate)


def _scan_loop(prep_fwd, prep_bwd, finish, refs, seq):
    mallf_ref, maskf_ref, mallb_ref, maskb_ref, oaccf_ref, oaccb_ref, stf_ref, stb_ref = refs
    nch = seq // CHUNK
    stf_ref[...] = jnp.zeros_like(stf_ref)
    stb_ref[...] = jnp.zeros_like(stb_ref)

    ngroups = nch // SCAN_UNROLL

    def rows(n):
        rows_f = [pl.ds(pl.multiple_of((n * SCAN_UNROLL + u) * CHUNK, CHUNK), CHUNK)
                  for u in range(SCAN_UNROLL)]
        rows_b = [pl.ds(pl.multiple_of((nch - 1 - n * SCAN_UNROLL - u) * CHUNK, CHUNK), CHUNK)
                  for u in range(SCAN_UNROLL)]
        return rows_f, rows_b

    def front(n):
        rows_f, rows_b = rows(n)
        return _scan_front([
            ([prep_fwd(rs) for rs in rows_f], mallf_ref, maskf_ref, False),
            ([prep_bwd(rs) for rs in rows_b], mallb_ref, maskb_ref, True)])

    def back(n, pend):
        rows_f, rows_b = rows(n)
        outs = _scan_back(pend, (stf_ref, stb_ref))
        for u in range(SCAN_UNROLL):
            oaccf_ref[rows_f[u], :] = outs[0][u]
            oaccb_ref[rows_b[u], :] = outs[1][u]

    def body(n, carry):
        back(n, front(n))
        return carry

    lax.fori_loop(0, ngroups, body, 0)

    def fin_body(n, carry):
        rs = pl.ds(pl.multiple_of(n * FIN_ROWS, FIN_ROWS), FIN_ROWS)
        finish(rs, oaccf_ref[rs, :] + oaccb_ref[rs, :])
        return carry

    lax.fori_loop(0, seq // FIN_ROWS, fin_body, 0)


def _hgrn_kernel(q_ref, zf_ref, zb_ref, i_ref, gate_ref, lb_ref, ng_ref,
                 mallf_ref, maskf_ref, mallb_ref, maskb_ref, o_ref, *scratch,
                 layer_idx, n_even, seq):
    lbp = lb_ref[...].astype(F32)
    rows = [[lbp[l * 2 + d:l * 2 + d + 1] for l in range(n_even)] for d in range(2)]
    lower = []
    for d in range(2):
        mx = functools.reduce(jnp.maximum, rows[d])
        ex = [jnp.exp(r - mx) for r in rows[d]]
        tot = functools.reduce(lambda a, b: a + b, ex)
        lbs = [e / tot for e in ex]
        cum = functools.reduce(lambda a, b: a + b, lbs[:layer_idx + 1])
        lower.append(cum - lbs[0])

    def prep(rs, z_ref, lb):
        sg = jax.nn.sigmoid(z_ref[0, rs, :].astype(F32))
        f = lb + (1.0 - lb) * sg
        g = jnp.log(jnp.maximum(f, MIN_GATE))
        k = (1.0 - lb) * (1.0 - sg)
        q = _silu(q_ref[0, rs, :].astype(F32))
        return q.astype(BF16), k.astype(BF16), i_ref[0, rs, :].astype(BF16), g

    def finish(rs, o):
        y = _group_norm_gate(o, ng_ref[...], gate_ref[0, rs, :].astype(F32))
        o_ref[0, rs, :] = y.astype(o_ref.dtype)

    _scan_loop(lambda rs: prep(rs, zf_ref, lower[0]), lambda rs: prep(rs, zb_ref, lower[1]),
               finish, (mallf_ref, maskf_ref, mallb_ref, maskb_ref, *scratch), seq)


def _scan_scratch(t, dk, dv):
    return [pltpu.VMEM((t, dv), F32), pltpu.VMEM((t, dv), F32),
            pltpu.VMEM((dv, dk), F32), pltpu.VMEM((dv, dk), F32)]


def _const_spec(a):
    nd = a.ndim
    return pl.BlockSpec(a.shape, lambda i, j: (0,) * nd)


def _hgrn(proj, lb_param, norm_g, layer_idx, consts):
    b, t, _ = proj.shape
    n_even = lb_param.shape[0]
    hd = HEAD_DIM_B
    col = lambda off: pl.BlockSpec((1, t, hd), lambda i, j: (i, 0, off // hd + j))
    return pl.pallas_call(
        functools.partial(_hgrn_kernel, layer_idx=layer_idx, n_even=n_even, seq=t),
        grid=(b, N_HEADS_B),
        in_specs=[col(E_QB), col(E_ZF), col(E_ZB), col(E_IB), col(E_GB),
                  pl.BlockSpec((n_even * 2, hd), lambda i, j: (0, j)),
                  pl.BlockSpec((1, hd), lambda i, j: (0, j)),
                  *[_const_spec(a) for a in consts]],
        out_specs=pl.BlockSpec((1, t, hd), lambda i, j: (i, 0, j)),
        out_shape=jax.ShapeDtypeStruct((b, t, W_B), BF16),
        scratch_shapes=_scan_scratch(t, hd, hd),
        compiler_params=_params(("parallel", "parallel")),
        name="hgrn",
    )(proj, proj, proj, proj, proj, lb_param.reshape(n_even * 2, W_B),
      norm_g.reshape(1, W_B), *consts)


def _gla_kernel(q_ref, k_ref, v_ref, gate_ref, r_ref, wuh_ref, wul_ref, bias_ref, ng_ref,
                mallf_ref, maskf_ref, mallb_ref, maskb_ref, o_ref, gf_ref, gb_ref, *scratch, seq):
    def gate_body(n, carry):
        rs = pl.ds(pl.multiple_of(n * FIN_ROWS, FIN_ROWS), FIN_ROWS)
        r = r_ref[0, rs, :]
        assert r.dtype == BF16
        x = _dot(r, wuh_ref[...]) + _dot(r, wul_ref[...]) + bias_ref[...]
        g = (jnp.minimum(x, 0.0) - jnp.log(1.0 + jnp.exp(-jnp.abs(x)))) * (1.0 / GATE_TEMP)
        gf_ref[rs, :] = g[:, :DK_C]
        gb_ref[rs, :] = g[:, DK_C:]
        return carry

    lax.fori_loop(0, seq // FIN_ROWS, gate_body, 0)

    def prep(rs, g_ref):
        q = (q_ref[0, rs, :].astype(F32) * (DK_C ** -0.5)).astype(BF16)
        return q, k_ref[0, rs, :].astype(BF16), v_ref[0, rs, :].astype(BF16), g_ref[rs, :]

    def finish(rs, o):
        y = _group_norm_gate(o, ng_ref[...], gate_ref[0, rs, :].astype(F32))
        o_ref[0, rs, :] = y.astype(o_ref.dtype)

    _scan_loop(lambda rs: prep(rs, gf_ref), lambda rs: prep(rs, gb_ref), finish,
               (mallf_ref, maskf_ref, mallb_ref, maskb_ref, *scratch), seq)


def _gla(proj, w_up, b_gate, norm_g, consts):
    b, t, _ = proj.shape
    dk, dv, nh = DK_C, DV_C, N_HEADS_C
    wuf = jnp.zeros((LANE, WK_C), F32).at[0:GATE_RANK].set(w_up[0])
    wub = jnp.zeros((LANE, WK_C), F32).at[GATE_RANK:2 * GATE_RANK].set(w_up[1])
    wcat = jnp.concatenate([wuf.reshape(LANE, nh, dk), wub.reshape(LANE, nh, dk)],
                           axis=2).reshape(LANE, 2 * WK_C)
    w_hi = wcat.astype(BF16)
    w_lo = (wcat - w_hi.astype(F32)).astype(BF16)
    bcat = jnp.concatenate([b_gate[0].reshape(nh, dk), b_gate[1].reshape(nh, dk)],
                           axis=1).reshape(1, 2 * WK_C)
    colk = lambda off: pl.BlockSpec((1, t, dk), lambda i, j: (i, 0, off // dk + j))
    colv = lambda off: pl.BlockSpec((1, t, dv), lambda i, j: (i, 0, off // dv + j))
    return pl.pallas_call(
        functools.partial(_gla_kernel, seq=t),
        grid=(b, nh),
        in_specs=[colk(O_QC), colk(O_KC), colv(O_VC), colv(O_GC),
                  pl.BlockSpec((1, t, LANE), lambda i, j: (i, 0, O_R // LANE)),
                  pl.BlockSpec((LANE, 2 * dk), lambda i, j: (0, j)),
                  pl.BlockSpec((LANE, 2 * dk), lambda i, j: (0, j)),
                  pl.BlockSpec((1, 2 * dk), lambda i, j: (0, j)),
                  pl.BlockSpec((1, dv), lambda i, j: (0, j)),
                  *[_const_spec(a) for a in consts]],
        out_specs=pl.BlockSpec((1, t, dv), lambda i, j: (i, 0, j)),
        out_shape=jax.ShapeDtypeStruct((b, t, WV_C), BF16),
        scratch_shapes=[pltpu.VMEM((t, dk), F32), pltpu.VMEM((t, dk), F32),
                        *_scan_scratch(t, dk, dv)],
        compiler_params=_params(("parallel", "parallel")),
        name="gla",
    )(proj, proj, proj, proj, proj, w_hi, w_lo, bcat, norm_g.reshape(1, WV_C), *consts)


def _even_w_in(w):
    return jnp.concatenate([w[..., 0:512], w[..., 768:4864], w[..., 512:768]],
                           axis=-1).astype(BF16)


def _odd_w_in(w):
    pad = jnp.zeros(w.shape[:-1] + (ODD_IN_PAD - 4128,), w.dtype)
    return jnp.concatenate([w[..., 0:3072], w[..., 3104:4128], w[..., 3072:3104], pad],
                           axis=-1).astype(BF16)


def kernel(x, mem, norm_even, w_in_even, sink, lb_param, hgrn_norm, w_out_even, norm_odd,
           w_in_odd, w_gate_up, b_gate, gla_norm, w_out_odd, mem_norm, w_mem_kv, final_norm):
    b, t, d = x.shape
    consts = (*_scan_constants(False), *_scan_constants(True))
    win_bias = _win_bias()
    kv_all = _memkv(mem, mem_norm, w_mem_kv.astype(BF16))
    w_in = (_even_w_in(w_in_even), _odd_w_in(w_in_odd))
    w_out = (w_out_even.astype(BF16), w_out_odd.astype(BF16))
    norm_in = (norm_even, norm_odd)

    x2 = x.reshape(b * t, d)
    _, proj = _boundary(x2, g_in=norm_in[0][0], w_in=(w_in[0], 0))
    for l in range(DEPTH):
        i = l // 2
        if l % 2 == 0:
            proj = proj.reshape(b, t, EVEN_IN)
            a = _winattn(proj, sink[i], win_bias)
            o = _hgrn(proj, lb_param, hgrn_norm[i], i, consts)
            mo = _memattn(proj, kv_all, l, E_QM, E_GM)
            parts = [a.reshape(b * t, W_A), o.reshape(b * t, W_B), mo.reshape(b * t, W_M)]
        else:
            proj = proj.reshape(b, t, ODD_IN_PAD)
            o = _gla(proj, w_gate_up[i], b_gate[i], gla_norm[i], consts)
            mo = _memattn(proj, kv_all, l, O_QM, O_GM)
            parts = [o.reshape(b * t, WV_C), mo.reshape(b * t, W_M)]
        if l == DEPTH - 1:
            x2, _ = _boundary(x2, parts, (w_out[l % 2], i), g_final=final_norm)
        else:
            nl = l + 1
            x2, proj = _boundary(x2, parts, (w_out[l % 2], i), g_in=norm_in[nl % 2][nl // 2],
                                 w_in=(w_in[nl % 2], nl // 2))
    return x2.reshape(b, t, d)
```

```python
import functools

import numpy as np
import jax
import jax.numpy as jnp
from jax import lax
from jax.experimental import pallas as pl
from jax.experimental.pallas import tpu as pltpu

F32 = jnp.float32
BF16 = jnp.bfloat16

D_MODEL = 1024
DEPTH = 4
N_Q_A, N_KV_A, HEAD_DIM_A = 8, 2, 64
W_A, W_KV_A = 512, 128
WINDOW = 128
BLOCK = 128
N_HEADS_B, HEAD_DIM_B, W_B = 4, 128, 512
N_HEADS_C, DK_C, DV_C = 4, 128, 256
WK_C, WV_C = 512, 1024
GATE_RANK = 16
GATE_TEMP = 16.0
N_MEM, N_HEADS_M, HEAD_DIM_M, W_M = 256, 4, 128, 512
EPS = 1e-6
MASK_VALUE = -1e30
MIN_GATE = 1e-30
LOG2E = 1.4426950408889634

EVEN_IN = 4864
E_QA, E_GA, E_QB, E_ZF, E_ZB, E_IB, E_GB, E_QM, E_GM, E_KA, E_VA = (
    0, 512, 1024, 1536, 2048, 2560, 3072, 3584, 4096, 4608, 4736)
ODD_IN_PAD = 4352
O_QC, O_KC, O_VC, O_GC, O_QM, O_GM, O_R = 0, 512, 1024, 2048, 3072, 3584, 4096

LANE = 128
PROJ_DTYPE = BF16
PROJ_TM = 512
PROJ_TN = 256
CHUNK = 64
N_LEVELS = 6
LEVEL_PAIRS = ((0, 1), (2, 3), (4, 5), (6, None))
MXU_SKEW = 2
MXU_LEVELS = 3
SCAN_UNROLL = 8
FIN_ROWS = 256
MEM_TQ = 512
VMEM_LIMIT = 56 * 1024 * 1024

_NT = (((1,), (1,)), ((), ()))
_TN = (((0,), (0,)), ((), ()))


def _dot(a, b):
    return jnp.dot(a, b, preferred_element_type=F32)


def _dot_nt(a, b):
    return lax.dot_general(a, b, _NT, preferred_element_type=F32)


def _dot_tn(a, b):
    return lax.dot_general(a, b, _TN, preferred_element_type=F32)


def _silu(x):
    return x * jax.nn.sigmoid(x)


def _params(sem):
    return pltpu.CompilerParams(dimension_semantics=sem, vmem_limit_bytes=VMEM_LIMIT)


def _boundary_kernel(*refs, n_parts, final, has_in):
    refs = list(refs)
    x_ref = refs.pop(0)
    parts = [refs.pop(0) for _ in range(n_parts)]
    w_out_ref = refs.pop(0) if n_parts else None
    gf_ref = refs.pop(0) if final else None
    g_ref, w_in_ref = (refs.pop(0), refs.pop(0)) if has_in else (None, None)
    xo_ref = refs.pop(0) if n_parts else None
    proj_ref = refs.pop(0) if has_in else None

    x = x_ref[...]
    off = 0
    for p in parts:
        wp = p.shape[1]
        x = x + _dot(p[...], w_out_ref[off:off + wp, :])
        off += wp
    if final:
        ms = jnp.mean(x * x, axis=-1, keepdims=True)
        x = x * lax.rsqrt(ms + EPS) * gf_ref[...]
    if n_parts:
        xo_ref[...] = x
    if has_in:
        ms = jnp.mean(x * x, axis=-1, keepdims=True)
        h = (x * lax.rsqrt(ms + EPS) * g_ref[...]).astype(BF16)
        for c in range(0, proj_ref.shape[1], PROJ_TN):
            proj_ref[:, c:c + PROJ_TN] = _dot(h, w_in_ref[:, c:c + PROJ_TN]).astype(proj_ref.dtype)


def _boundary(x2, parts=(), w_out=None, g_final=None, g_in=None, w_in=None):
    n, d = x2.shape
    final, has_in = g_final is not None, w_in is not None
    row = lambda width: pl.BlockSpec((PROJ_TM, width), lambda i: (i, 0))
    whole = lambda a: pl.BlockSpec(a.shape, lambda i: (0, 0))
    layer = lambda a, l: pl.BlockSpec((None,) + a.shape[1:], lambda i: (l, 0, 0))
    args, in_specs = [x2], [row(d)]
    for p in parts:
        args.append(p)
        in_specs.append(row(p.shape[1]))
    if parts:
        args.append(w_out[0])
        in_specs.append(layer(*w_out))
    if final:
        args.append(g_final.reshape(1, d))
        in_specs.append(whole(args[-1]))
    out_shape, out_specs = [], []
    if parts:
        out_shape.append(jax.ShapeDtypeStruct((n, d), F32))
        out_specs.append(row(d))
    if has_in:
        n_out = w_in[0].shape[2]
        args += [g_in.reshape(1, d), w_in[0]]
        in_specs += [whole(args[-2]), layer(*w_in)]
        out_shape.append(jax.ShapeDtypeStruct((n, n_out), PROJ_DTYPE))
        out_specs.append(row(n_out))
    outs = pl.pallas_call(
        functools.partial(_boundary_kernel, n_parts=len(parts), final=final, has_in=has_in),
        grid=(n // PROJ_TM,),
        in_specs=in_specs,
        out_specs=out_specs,
        out_shape=out_shape,
        compiler_params=_params(("parallel",)),
        name="boundary",
    )(*args)
    outs = list(outs)
    return (outs.pop(0) if parts else None), (outs.pop(0) if has_in else None)


def _memkv_kernel(m_ref, g_ref, w_ref, o_ref):
    x = m_ref[0]
    ms = jnp.mean(x * x, axis=-1, keepdims=True)
    h = (x * lax.rsqrt(ms + EPS) * g_ref[...]).astype(BF16)
    o_ref[0, 0] = _dot(h, w_ref[0]).astype(o_ref.dtype)


def _memkv(mem, g, w_kv):
    b, nm, d = mem.shape
    depth, _, wkv = w_kv.shape
    return pl.pallas_call(
        _memkv_kernel,
        grid=(depth, b),
        in_specs=[
            pl.BlockSpec((1, nm, d), lambda l, i: (i, 0, 0)),
            pl.BlockSpec((1, d), lambda l, i: (0, 0)),
            pl.BlockSpec((1, d, wkv), lambda l, i: (l, 0, 0)),
        ],
        out_specs=pl.BlockSpec((1, 1, nm, wkv), lambda l, i: (l, i, 0, 0)),
        out_shape=jax.ShapeDtypeStruct((depth, b, nm, wkv), BF16),
        compiler_params=_params(("parallel", "parallel")),
        name="memkv",
    )(mem, g.reshape(1, d), w_kv)


def _memattn_kernel(q_ref, g_ref, kv_ref, o_ref):
    q = q_ref[0].astype(F32) * (HEAD_DIM_M ** -0.5)
    outs = []
    for h in range(N_HEADS_M):
        sl = slice(h * HEAD_DIM_M, (h + 1) * HEAD_DIM_M)
        qh = q[:, sl].astype(BF16)
        kh = kv_ref[0, 0, :, h * HEAD_DIM_M:(h + 1) * HEAD_DIM_M]
        vh = kv_ref[0, 0, :, W_M + h * HEAD_DIM_M:W_M + (h + 1) * HEAD_DIM_M]
        s = _dot_nt(qh, kh)
        m = jnp.max(s, axis=-1, keepdims=True)
        p = jnp.exp(s - m)
        l = jnp.sum(p, axis=-1, keepdims=True)
        outs.append(_dot(p.astype(BF16), vh) / l)
    o = jnp.concatenate(outs, axis=-1)
    o_ref[0] = (o * _silu(g_ref[0].astype(F32))).astype(o_ref.dtype)


def _memattn(proj, kv, layer, q_off, g_off):
    b, t, _ = proj.shape
    qb, gb = q_off // W_M, g_off // W_M
    return pl.pallas_call(
        _memattn_kernel,
        grid=(b, t // MEM_TQ),
        in_specs=[
            pl.BlockSpec((1, MEM_TQ, W_M), lambda i, j: (i, j, qb)),
            pl.BlockSpec((1, MEM_TQ, W_M), lambda i, j: (i, j, gb)),
            pl.BlockSpec((1, 1, N_MEM, 2 * W_M), lambda i, j: (layer, i, 0, 0)),
        ],
        out_specs=pl.BlockSpec((1, MEM_TQ, W_M), lambda i, j: (i, j, 0)),
        out_shape=jax.ShapeDtypeStruct((b, t, W_M), BF16),
        compiler_params=_params(("parallel", "parallel")),
        name="memattn",
    )(proj, proj, kv)


def _win_bias():
    i = np.arange(BLOCK)[:, None]
    j = np.arange(BLOCK)[None, :]
    out = np.empty((3, N_Q_A, 3, BLOCK, BLOCK), np.float32)
    for e in range(3):
        for h in range(N_Q_A):
            slope = 2.0 ** (-8.0 * (h + 1) / N_Q_A)
            for blk in range(3):
                dist = np.abs(i - j - (blk - 1) * BLOCK)
                valid = dist <= WINDOW
                if (e == 0 and blk == 0) or (e == 2 and blk == 2):
                    valid = np.zeros_like(valid)
                out[e, h, blk] = np.where(valid, -slope * dist * LOG2E, MASK_VALUE)
    return jnp.asarray(out)


def _winattn_kernel(sink_ref, q_ref, g_ref, k_ref, v_ref, bias_ref, o_ref, *, seq):
    c = pl.program_id(1)
    nb = seq // BLOCK
    hd = HEAD_DIM_A
    lo = lax.broadcasted_iota(jnp.int32, (BLOCK, LANE), 1) < hd
    q = (q_ref[0].astype(F32) * (hd ** -0.5 * LOG2E)).astype(BF16)
    kvar, vvar = [], []
    for off in (-1, 0, 1):
        start = pl.multiple_of(jnp.clip(c + off, 0, nb - 1) * BLOCK, BLOCK)
        kb = k_ref[0, pl.ds(start, BLOCK), :].astype(F32)
        vb = v_ref[0, pl.ds(start, BLOCK), :].astype(F32)
        kr = pltpu.roll(kb, hd, axis=1)
        vr = pltpu.roll(vb, hd, axis=1)
        kvar.append({(0, 0): jnp.where(lo, kb, 0.0).astype(BF16), (0, 1): jnp.where(lo, 0.0, kr).astype(BF16),
                     (1, 0): jnp.where(lo, kr, 0.0).astype(BF16), (1, 1): jnp.where(lo, 0.0, kb).astype(BF16)})
        vvar.append({(0, 0): jnp.where(lo, vb, 1.0).astype(BF16), (0, 1): jnp.where(lo, 1.0, vr).astype(BF16),
                     (1, 0): jnp.where(lo, vr, 1.0).astype(BF16), (1, 1): jnp.where(lo, 1.0, vb).astype(BF16)})
    groups = [(n, par) for n in range(N_KV_A) for par in range(2)]
    s = {}
    for n, par in groups:
        lhs = jnp.concatenate([q[:, (2 * n) * LANE:(2 * n + 1) * LANE],
                               q[:, (2 * n + 1) * LANE:(2 * n + 2) * LANE]], axis=0)
        for blk in range(3):
            s[n, par, blk] = _dot_nt(lhs, kvar[blk][n, par])
    p, sink_term = {}, {}
    for n, par in groups:
        for r in range(2):
            h = 4 * n + 2 * r + par
            sink2 = sink_ref[h] * LOG2E
            sb = [s[n, par, blk][r * BLOCK:(r + 1) * BLOCK] + bias_ref[0, h, blk] for blk in range(3)]
            m = jnp.max(jnp.maximum(jnp.maximum(sb[0], sb[1]), sb[2]), axis=-1, keepdims=True)
            m = jnp.maximum(m, sink2)
            p[h] = [jnp.exp2(x - m).astype(BF16) for x in sb]
            sink_term[h] = jnp.exp2(sink2 - m)
    res = {}
    for n, par in groups:
        acc = None
        for blk in range(3):
            lhs = jnp.concatenate([p[4 * n + par][blk], p[4 * n + 2 + par][blk]], axis=0)
            d = _dot(lhs, vvar[blk][n, par])
            acc = d if acc is None else acc + d
        res[n, par] = acc
    for j in range(N_Q_A // 2):
        n, r = j // 2, j % 2
        r_even = res[n, 0][r * BLOCK:(r + 1) * BLOCK]
        r_odd = res[n, 1][r * BLOCK:(r + 1) * BLOCK]
        vals = jnp.where(lo, r_even, r_odd)
        sums = pltpu.roll(jnp.where(lo, r_odd, r_even), hd, axis=1)
        denom = sums + jnp.where(lo, sink_term[2 * j], sink_term[2 * j + 1])
        gate = g_ref[0, :, j * LANE:(j + 1) * LANE].astype(F32)
        o_ref[0, :, j * LANE:(j + 1) * LANE] = (vals / denom * _silu(gate)).astype(o_ref.dtype)


def _winattn(proj, sink, bias):
    b, t, _ = proj.shape
    nb = t // BLOCK
    assert nb >= 2
    edge = lambda i, j: (jnp.where(j == 0, 0, jnp.where(j == nb - 1, 2, 1)), 0, 0, 0, 0)
    return pl.pallas_call(
        functools.partial(_winattn_kernel, seq=t),
        grid=(b, nb),
        in_specs=[
            pl.BlockSpec(memory_space=pltpu.SMEM),
            pl.BlockSpec((1, BLOCK, W_A), lambda i, j: (i, j, E_QA // W_A)),
            pl.BlockSpec((1, BLOCK, W_A), lambda i, j: (i, j, E_GA // W_A)),
            pl.BlockSpec((1, t, W_KV_A), lambda i, j: (i, 0, E_KA // W_KV_A)),
            pl.BlockSpec((1, t, W_KV_A), lambda i, j: (i, 0, E_VA // W_KV_A)),
            pl.BlockSpec((1, N_Q_A, 3, BLOCK, BLOCK), edge),
        ],
        out_specs=pl.BlockSpec((1, BLOCK, W_A), lambda i, j: (i, j, 0)),
        out_shape=jax.ShapeDtypeStruct((b, t, W_A), BF16),
        compiler_params=_params(("parallel", "arbitrary")),
        name="winattn",
    )(sink, proj, proj, proj, proj, bias)


def _scan_constants(rev):
    c = CHUNK
    t = np.arange(c)[:, None]
    u = np.arange(c)[None, :]
    mats = []
    mats.append((u >= t) if rev else (u <= t))
    masks = [t == u]
    for lvl in range(1, N_LEVELS + 1):
        blk, half = 2 ** lvl, 2 ** (lvl - 1)
        m = (t // blk) * blk + half
        if rev:
            mat = np.where(t < m, (u >= t) & (u < m), (u >= m) & (u < t))
            mask = (t // blk == u // blk) & (t % blk < half) & (u % blk >= half)
        else:
            mat = np.where(t >= m, (u >= m) & (u <= t), (u > t) & (u < m))
            mask = (t // blk == u // blk) & (t % blk >= half) & (u % blk < half)
        if lvl <= MXU_LEVELS:
            mats.append(mat)
        masks.append(mask)
    mall = np.concatenate(mats, axis=0).astype(np.float32)
    mall2 = np.concatenate([mall, mall], axis=1)
    none = np.zeros((c, c), bool)
    masks2 = np.stack([np.concatenate([masks[la], none if lb is None else masks[lb]], axis=1)
                       for la, lb in LEVEL_PAIRS]).astype(np.float32)
    return jnp.asarray(mall2, dtype=BF16), jnp.asarray(masks2)


def _level_exponent(b, lvl, rev):
    blk, half = 2 ** lvl, 2 ** (lvl - 1)
    parts = []
    for s in range(0, b.shape[0], blk):
        m = s + half
        lower, upper = b[s:m], b[m:s + blk]
        if rev:
            r = b[m:m + 1]
            parts += [lower - r, r - upper]
        else:
            r = b[m - 1:m]
            parts += [r - lower, upper - r]
    return jnp.concatenate(parts, axis=0)


def _split2(g):
    g1 = g.astype(BF16)
    g2 = (g - g1.astype(F32)).astype(BF16)
    return jnp.concatenate([g1, g2], axis=0)


def _scan_front(dirs):
    c = CHUNK
    e_tots, es = {}, {}
    for di, (chunks, mall2_ref, _, rev) in enumerate(dirs):
        g12 = jnp.concatenate([_split2(g) for (_, _, _, g) in chunks], axis=1)
        ex = _dot(mall2_ref[...], g12)
        for u, (q, _, _, _) in enumerate(chunks):
            dk = q.shape[1]
            exu = ex[:, u * dk:(u + 1) * dk]
            b = exu[0:c]
            b_tot = b[0:1] if rev else b[c - 1:c]
            full = jnp.concatenate(
                [exu] + [_level_exponent(b, lvl, rev) for lvl in range(MXU_LEVELS + 1, N_LEVELS + 1)]
                + [b_tot - b], axis=0)
            e_tots[di, u] = jnp.exp2(b_tot)
            es[di, u] = jnp.exp2(full).astype(BF16)

    def e_rows(di, u, blk):
        return es[di, u][blk * c:(blk + 1) * c]

    o_intra, w_inc, a2s = {}, {}, {}
    chains = [(di, u) for u in range(SCAN_UNROLL) for di in range(len(dirs))]

    def scores(di, u):
        q, k, v, _ = dirs[di][0][u]
        mask_ref = dirs[di][2]
        zeros = jnp.zeros_like(k)
        a2 = None
        for pi, (la, lb) in enumerate(LEVEL_PAIRS):
            qa, ka = (q, k) if la == 0 else (q * e_rows(di, u, la), k * e_rows(di, u, la))
            if lb is None:
                lhs, rhs = qa, jnp.concatenate([ka, zeros], axis=0)
            else:
                e_l = e_rows(di, u, lb)
                lhs = jnp.concatenate([qa, q * e_l], axis=1)
                rhs = jnp.concatenate([jnp.concatenate([ka, zeros], axis=1),
                                       jnp.concatenate([zeros, k * e_l], axis=1)], axis=0)
            p = mask_ref[pi] * _dot_nt(lhs, rhs)
            a2 = p if a2 is None else a2 + p
        a2s[di, u] = a2
        w_inc[di, u] = _dot_tn(v, k * e_rows(di, u, N_LEVELS + 1))

    def intra(di, u):
        v = dirs[di][0][u][2]
        o_intra[di, u] = _dot(a2s[di, u].astype(BF16), jnp.concatenate([v, v], axis=0))

    for i in range(len(chains) + MXU_SKEW):
        if i < len(chains):
            scores(*chains[i])
        if i >= MXU_SKEW:
            intra(*chains[i - MXU_SKEW])
    return [[(dirs[di][0][u][0] * e_rows(di, u, 0), o_intra[di, u], w_inc[di, u], e_tots[di, u])
             for u in range(SCAN_UNROLL)] for di in range(len(dirs))]


def _scan_back(pend, st_refs):
    sts = [r[...] for r in st_refs]
    outs = [[None] * SCAN_UNROLL for _ in st_refs]
    for u in range(SCAN_UNROLL):
        for di in range(len(st_refs)):
            qe, o_intra, w_inc, e_tot = pend[di][u]
            st = sts[di]
            outs[di][u] = o_intra + _dot_nt(qe, st.astype(BF16))
            sts[di] = st * e_tot + w_inc
    for r, st in zip(st_refs, sts):
        r[...] = st
    return outs


def _group_norm_gate(o, norm_g, gate):
    ms = jnp.mean(o * o, axis=-1, keepdims=True)
    return o * lax.rsqrt(ms + EPS) * norm_g * _silu(gate)


def _scan_loop(prep_fwd, prep_bwd, finish, refs, seq):
    mallf_ref, maskf_ref, mallb_ref, maskb_ref, oaccf_ref, oaccb_ref, stf_ref, stb_ref = refs
    nch = seq // CHUNK
    stf_ref[...] = jnp.zeros_like(stf_ref)
    stb_ref[...] = jnp.zeros_like(stb_ref)

    ngroups = nch // SCAN_UNROLL

    def rows(n):
        rows_f = [pl.ds(pl.multiple_of((n * SCAN_UNROLL + u) * CHUNK, CHUNK), CHUNK)
                  for u in range(SCAN_UNROLL)]
        rows_b = [pl.ds(pl.multiple_of((nch - 1 - n * SCAN_UNROLL - u) * CHUNK, CHUNK), CHUNK)
                  for u in range(SCAN_UNROLL)]
        return rows_f, rows_b

    def front(n):
        rows_f, rows_b = rows(n)
        return _scan_front([
            ([prep_fwd(rs) for rs in rows_f], mallf_ref, maskf_ref, False),
            ([prep_bwd(rs) for rs in rows_b], mallb_ref, maskb_ref, True)])

    def back(n, pend):
        rows_f, rows_b = rows(n)
        outs = _scan_back(pend, (stf_ref, stb_ref))
        for u in range(SCAN_UNROLL):
            oaccf_ref[rows_f[u], :] = outs[0][u]
            oaccb_ref[rows_b[u], :] = outs[1][u]

    def body(n, carry):
        back(n, front(n))
        return carry

    lax.fori_loop(0, ngroups, body, 0)

    def fin_body(n, carry):
        rs = pl.ds(pl.multiple_of(n * FIN_ROWS, FIN_ROWS), FIN_ROWS)
        finish(rs, oaccf_ref[rs, :] + oaccb_ref[rs, :])
        return carry

    lax.fori_loop(0, seq // FIN_ROWS, fin_body, 0, unroll=2)


def _hgrn_kernel(q_ref, zf_ref, zb_ref, i_ref, gate_ref, lb_ref, ng_ref,
                 mallf_ref, maskf_ref, mallb_ref, maskb_ref, o_ref, *scratch,
                 layer_idx, n_even, seq):
    lbp = lb_ref[...].astype(F32)
    rows = [[lbp[l * 2 + d:l * 2 + d + 1] for l in range(n_even)] for d in range(2)]
    lower = []
    for d in range(2):
        mx = functools.reduce(jnp.maximum, rows[d])
        ex = [jnp.exp(r - mx) for r in rows[d]]
        tot = functools.reduce(lambda a, b: a + b, ex)
        lbs = [e / tot for e in ex]
        cum = functools.reduce(lambda a, b: a + b, lbs[:layer_idx + 1])
        lower.append(cum - lbs[0])

    def prep(rs, z_ref, lb):
        sg = jax.nn.sigmoid(z_ref[0, rs, :].astype(F32))
        f = lb + (1.0 - lb) * sg
        g = jnp.log2(jnp.maximum(f, MIN_GATE))
        k = (1.0 - lb) * (1.0 - sg)
        q = _silu(q_ref[0, rs, :].astype(F32))
        return q.astype(BF16), k.astype(BF16), i_ref[0, rs, :].astype(BF16), g

    def finish(rs, o):
        y = _group_norm_gate(o, ng_ref[...], gate_ref[0, rs, :].astype(F32))
        o_ref[0, rs, :] = y.astype(o_ref.dtype)

    _scan_loop(lambda rs: prep(rs, zf_ref, lower[0]), lambda rs: prep(rs, zb_ref, lower[1]),
               finish, (mallf_ref, maskf_ref, mallb_ref, maskb_ref, *scratch), seq)


def _scan_scratch(t, dk, dv):
    return [pltpu.VMEM((t, dv), F32), pltpu.VMEM((t, dv), F32),
            pltpu.VMEM((dv, dk), F32), pltpu.VMEM((dv, dk), F32)]


def _const_spec(a):
    nd = a.ndim
    return pl.BlockSpec(a.shape, lambda i, j: (0,) * nd)


def _hgrn(proj, lb_param, norm_g, layer_idx, consts):
    b, t, _ = proj.shape
    n_even = lb_param.shape[0]
    hd = HEAD_DIM_B
    col = lambda off: pl.BlockSpec((1, t, hd), lambda i, j: (i, 0, off // hd + j))
    return pl.pallas_call(
        functools.partial(_hgrn_kernel, layer_idx=layer_idx, n_even=n_even, seq=t),
        grid=(b, N_HEADS_B),
        in_specs=[col(E_QB), col(E_ZF), col(E_ZB), col(E_IB), col(E_GB),
                  pl.BlockSpec((n_even * 2, hd), lambda i, j: (0, j)),
                  pl.BlockSpec((1, hd), lambda i, j: (0, j)),
                  *[_const_spec(a) for a in consts]],
        out_specs=pl.BlockSpec((1, t, hd), lambda i, j: (i, 0, j)),
        out_shape=jax.ShapeDtypeStruct((b, t, W_B), BF16),
        scratch_shapes=_scan_scratch(t, hd, hd),
        compiler_params=_params(("parallel", "parallel")),
        name="hgrn",
    )(proj, proj, proj, proj, proj, lb_param.reshape(n_even * 2, W_B),
      norm_g.reshape(1, W_B), *consts)


def _gla_kernel(q_ref, k_ref, v_ref, gate_ref, r_ref, wuh_ref, wul_ref, bias_ref, ng_ref,
                mallf_ref, maskf_ref, mallb_ref, maskb_ref, o_ref, gf_ref, gb_ref, *scratch, seq):
    def gate_body(n, carry):
        rs = pl.ds(pl.multiple_of(n * FIN_ROWS, FIN_ROWS), FIN_ROWS)
        r = r_ref[0, rs, :]
        assert r.dtype == BF16
        x2 = _dot(r, wuh_ref[...]) + _dot(r, wul_ref[...]) + bias_ref[...]
        g = (jnp.minimum(x2, 0.0) - jnp.log2(1.0 + jnp.exp2(-jnp.abs(x2)))) * (1.0 / GATE_TEMP)
        gf_ref[rs, :] = g[:, :DK_C]
        gb_ref[rs, :] = g[:, DK_C:]
        return carry

    lax.fori_loop(0, seq // FIN_ROWS, gate_body, 0, unroll=4)

    def prep(rs, g_ref):
        q = (q_ref[0, rs, :].astype(F32) * (DK_C ** -0.5)).astype(BF16)
        return q, k_ref[0, rs, :].astype(BF16), v_ref[0, rs, :].astype(BF16), g_ref[rs, :]

    def finish(rs, o):
        y = _group_norm_gate(o, ng_ref[...], gate_ref[0, rs, :].astype(F32))
        o_ref[0, rs, :] = y.astype(o_ref.dtype)

    _scan_loop(lambda rs: prep(rs, gf_ref), lambda rs: prep(rs, gb_ref), finish,
               (mallf_ref, maskf_ref, mallb_ref, maskb_ref, *scratch), seq)


def _gla(proj, w_up, b_gate, norm_g, consts):
    b, t, _ = proj.shape
    dk, dv, nh = DK_C, DV_C, N_HEADS_C
    wuf = jnp.zeros((LANE, WK_C), F32).at[0:GATE_RANK].set(w_up[0])
    wub = jnp.zeros((LANE, WK_C), F32).at[GATE_RANK:2 * GATE_RANK].set(w_up[1])
    wcat = jnp.concatenate([wuf.reshape(LANE, nh, dk), wub.reshape(LANE, nh, dk)],
                           axis=2).reshape(LANE, 2 * WK_C) * LOG2E
    w_hi = wcat.astype(BF16)
    w_lo = (wcat - w_hi.astype(F32)).astype(BF16)
    bcat = jnp.concatenate([b_gate[0].reshape(nh, dk), b_gate[1].reshape(nh, dk)],
                           axis=1).reshape(1, 2 * WK_C) * LOG2E
    colk = lambda off: pl.BlockSpec((1, t, dk), lambda i, j: (i, 0, off // dk + j))
    colv = lambda off: pl.BlockSpec((1, t, dv), lambda i, j: (i, 0, off // dv + j))
    return pl.pallas_call(
        functools.partial(_gla_kernel, seq=t),
        grid=(b, nh),
        in_specs=[colk(O_QC), colk(O_KC), colv(O_VC), colv(O_GC),
                  pl.BlockSpec((1, t, LANE), lambda i, j: (i, 0, O_R // LANE)),
                  pl.BlockSpec((LANE, 2 * dk), lambda i, j: (0, j)),
                  pl.BlockSpec((LANE, 2 * dk), lambda i, j: (0, j)),
                  pl.BlockSpec((1, 2 * dk), lambda i, j: (0, j)),
                  pl.BlockSpec((1, dv), lambda i, j: (0, j)),
                  *[_const_spec(a) for a in consts]],
        out_specs=pl.BlockSpec((1, t, dv), lambda i, j: (i, 0, j)),
        out_shape=jax.ShapeDtypeStruct((b, t, WV_C), BF16),
        scratch_shapes=[pltpu.VMEM((t, dk), F32), pltpu.VMEM((t, dk), F32),
                        *_scan_scratch(t, dk, dv)],
        compiler_params=_params(("parallel", "parallel")),
        name="gla",
    )(proj, proj, proj, proj, proj, w_hi, w_lo, bcat, norm_g.reshape(1, WV_C), *consts)


def _even_w_in(w):
    return jnp.concatenate([w[..., 0:512], w[..., 768:4864], w[..., 512:768]],
                           axis=-1).astype(BF16)


def _odd_w_in(w):
    pad = jnp.zeros(w.shape[:-1] + (ODD_IN_PAD - 4128,), w.dtype)
    return jnp.concatenate([w[..., 0:3072], w[..., 3104:4128], w[..., 3072:3104], pad],
                           axis=-1).astype(BF16)


def kernel(x, mem, norm_even, w_in_even, sink, lb_param, hgrn_norm, w_out_even, norm_odd,
           w_in_odd, w_gate_up, b_gate, gla_norm, w_out_odd, mem_norm, w_mem_kv, final_norm):
    b, t, d = x.shape
    consts = (*_scan_constants(False), *_scan_constants(True))
    win_bias = _win_bias()
    kv_all = _memkv(mem, mem_norm, w_mem_kv.astype(BF16))
    w_in = (_even_w_in(w_in_even), _odd_w_in(w_in_odd))
    w_out = (w_out_even.astype(BF16), w_out_odd.astype(BF16))
    norm_in = (norm_even, norm_odd)

    x2 = x.reshape(b * t, d)
    _, proj = _boundary(x2, g_in=norm_in[0][0], w_in=(w_in[0], 0))
    for l in range(DEPTH):
        i = l // 2
        if l % 2 == 0:
            proj = proj.reshape(b, t, EVEN_IN)
            a = _winattn(proj, sink[i], win_bias)
            o = _hgrn(proj, lb_param, hgrn_norm[i], i, consts)
            mo = _memattn(proj, kv_all, l, E_QM, E_GM)
            parts = [a.reshape(b * t, W_A), o.reshape(b * t, W_B), mo.reshape(b * t, W_M)]
        else:
            proj = proj.reshape(b, t, ODD_IN_PAD)
            o = _gla(proj, w_gate_up[i], b_gate[i], gla_norm[i], consts)
            mo = _memattn(proj, kv_all, l, O_QM, O_GM)
            parts = [o.reshape(b * t, WV_C), mo.reshape(b * t, W_M)]
        if l == DEPTH - 1:
            x2, _ = _boundary(x2, parts, (w_out[l % 2], i), g_final=final_norm)
        else:
            nl = l + 1
            x2, proj = _boundary(x2, parts, (w_out[l % 2], i), g_in=norm_in[nl % 2][nl // 2],
                                 w_in=(w_in[nl % 2], nl // 2))
    return x2.reshape(b, t, d)
```
